```python
import jax, jax.numpy as jnp
from jax import lax
import numpy as np

D_MODEL = 1024
BATCH = 16
SEQ = 4096
DEPTH = 2

N_MEM = 256
EPS = 1e-6
F32 = jnp.float32
CHUNK = 64
CONV_WIDTH = 3
M_HEADS = 4
M_HEAD_DIM = D_MODEL // 8
M_WIDTH = M_HEADS * M_HEAD_DIM
G_HEADS = 4
G_KEY_DIM = D_MODEL // 16
G_VAL_DIM = D_MODEL // 8
G_KEY_WIDTH = G_HEADS * G_KEY_DIM
G_VAL_WIDTH = G_HEADS * G_VAL_DIM
G_DECAY_RANK = 16
G_DECAY_TAU = 16.0
EVEN_SPLITS = (M_WIDTH, M_WIDTH, M_WIDTH, M_WIDTH, 2 * M_HEADS, 2 * M_HEADS, G_KEY_WIDTH, G_KEY_WIDTH, G_VAL_WIDTH, G_VAL_WIDTH, 2 * G_DECAY_RANK)
EVEN_IN_WIDTH = sum(EVEN_SPLITS)
EVEN_OUT_WIDTH = M_WIDTH + G_VAL_WIDTH
R_HEAD_DIM = 64
R_HEADS = D_MODEL // R_HEAD_DIM
R_DECAY_RANK = max(32, int(round(1.8 * D_MODEL ** 0.5 / 32)) * 32)
R_A_RANK = max(32, int(round(1.8 * D_MODEL ** 0.5 / 32)) * 32)
R_GATE_RANK = max(32, int(round(0.6 * D_MODEL ** 0.8 / 32)) * 32)
R_GN_EPS = 64e-5
X_HEADS = 4
X_HEAD_DIM = D_MODEL // X_HEADS
D_FF = -(-8 * D_MODEL // (3 * 256)) * 256

kernel_name = 'hybrid_mlstm_gla_rwkv7_encoder'


def rms_norm(x, g):
    xf = x.astype(F32)
    y = xf * lax.rsqrt(jnp.mean(xf * xf, axis=-1, keepdims=True) + EPS)
    return (y * g.astype(F32)).astype(x.dtype)


def head_rms(x, g):
    B, T, H, d = x.shape
    xf = x.astype(F32)
    y = xf * lax.rsqrt(jnp.mean(xf * xf, axis=-1, keepdims=True) + EPS)
    return y.reshape(B, T, H * d) * g.astype(F32)


def centred_depthwise_conv(x, w):
    K, C = w.shape
    return lax.conv_general_dilated(x, w[:, None, :].astype(x.dtype), window_strides=(1,), padding=[(K // 2, K // 2)], dimension_numbers=('NWC', 'WIO', 'NWC'), feature_group_count=C)


def _both(fwd, bwd):
    return jnp.stack([fwd, jnp.flip(bwd, axis=1)], axis=0)


def _to_chunks(a):
    n, b, t, h = a.shape[:4]
    a = a.reshape((n, b, t // CHUNK, CHUNK, h) + a.shape[4:])
    return jnp.moveaxis(a, (2, 4), (0, 3))


def _from_chunks(a):
    a = jnp.moveaxis(a, (0, 3), (2, 4))
    n, b, nc, l, h = a.shape[:5]
    return a.reshape((n, b, nc * l, h) + a.shape[5:])


def mlstm_bidirectional(q, k, v, ig, lf):
    _, B, _, H, d = q.shape
    mask = jnp.tril(jnp.ones((CHUNK, CHUNK), dtype=bool))

    def body(carry, xs):
        C, n, m = carry
        qc, kc, vc, igc, lfc = xs
        b = jnp.cumsum(lfc, axis=-1)
        g = b[..., -1]
        D = jnp.where(mask, b[..., :, None] - b[..., None, :] + igc[..., None, :], -jnp.inf)
        inter_log = b + m[..., None]
        m_t = jnp.maximum(inter_log, jnp.max(D, axis=-1))
        sc = jnp.einsum('nbhtk,nbhsk->nbhts', qc, kc) * jnp.exp(D - m_t[..., None])
        w_inter = jnp.exp(inter_log - m_t)
        num = jnp.einsum('nbhts,nbhsv->nbhtv', sc, vc) + w_inter[..., None] * jnp.einsum('nbhtk,nbhkv->nbhtv', qc, C)
        den = jnp.sum(sc, axis=-1) + w_inter * jnp.einsum('nbhtk,nbhk->nbht', qc, n)
        h = num / jnp.maximum(jnp.abs(den), jnp.exp(-m_t))[..., None]
        kw_log = g[..., None] - b + igc
        m_new = jnp.maximum(g + m, jnp.max(kw_log, axis=-1))
        kw = jnp.exp(kw_log - m_new[..., None])
        decay = jnp.exp(g + m - m_new)
        C = decay[..., None, None] * C + jnp.einsum('nbhsk,nbhsv->nbhkv', kc * kw[..., None], vc)
        n = decay[..., None] * n + jnp.einsum('nbhsk,nbhs->nbhk', kc, kw)
        return (C, n, m_new), h

    init = (jnp.zeros((2, B, H, d, d), F32), jnp.zeros((2, B, H, d), F32), jnp.zeros((2, B, H), F32))
    _, h = lax.scan(body, init, (_to_chunks(q), _to_chunks(k), _to_chunks(v), _to_chunks(ig), _to_chunks(lf)))
    return _from_chunks(h)


def gla_bidirectional(q, k, v, la):
    _, B, _, H, dk = q.shape
    dv = v.shape[-1]
    mask = jnp.tril(jnp.ones((CHUNK, CHUNK), dtype=bool))

    def body(S, xs):
        qc, kc, vc, lac = xs
        b = jnp.cumsum(lac, axis=-2)
        g = b[..., -1, :]
        rel = jnp.where(mask[:, :, None], b[..., :, None, :] - b[..., None, :, :], -jnp.inf)
        A = jnp.einsum('nbhtk,nbhsk,nbhtsk->nbhts', qc, kc, jnp.exp(rel))
        o = jnp.einsum('nbhts,nbhsv->nbhtv', A, vc) + jnp.einsum('nbhtk,nbhkv->nbhtv', qc * jnp.exp(b), S)
        S = jnp.exp(g)[..., None] * S + jnp.einsum('nbhsk,nbhsv->nbhkv', kc * jnp.exp(g[..., None, :] - b), vc)
        return S, o

    _, o = lax.scan(body, jnp.zeros((2, B, H, dk, dv), F32), (_to_chunks(q), _to_chunks(k), _to_chunks(v), _to_chunks(la)))
    return _from_chunks(o)


def even_mixer(h, w_in, conv_qk, m_ig_bias, m_fg_bias, m_norm, g_decay_w2, g_decay_b, g_norm, w_out):
    B, T, _ = h.shape
    split_at = np.cumsum(EVEN_SPLITS)[:-1].tolist()
    mq, mk, mv, mo, mi, mf, gq, gk, gv, gg, glr = jnp.split(h @ w_in, split_at, axis=-1)
    qk = jax.nn.silu(centred_depthwise_conv(jnp.concatenate([mq, mk], axis=-1), conv_qk))
    mq, mk = jnp.split(qk, 2, axis=-1)
    q = mq.astype(F32).reshape(B, T, M_HEADS, M_HEAD_DIM) * M_HEAD_DIM ** -0.5
    k = mk.astype(F32).reshape(B, T, M_HEADS, M_HEAD_DIM)
    v = mv.astype(F32).reshape(B, T, M_HEADS, M_HEAD_DIM)
    ig = mi.astype(F32).reshape(B, T, 2, M_HEADS) + m_ig_bias.astype(F32)
    lf = jax.nn.log_sigmoid(mf.astype(F32).reshape(B, T, 2, M_HEADS) + m_fg_bias.astype(F32))
    hm = mlstm_bidirectional(_both(q, q), _both(k, k), _both(v, v), _both(ig[:, :, 0], ig[:, :, 1]), _both(lf[:, :, 0], lf[:, :, 1]))
    hm = hm[0] + jnp.flip(hm[1], axis=1)
    hm = head_rms(hm, m_norm) * jax.nn.sigmoid(mo.astype(F32))
    gqh = gq.astype(F32).reshape(B, T, G_HEADS, G_KEY_DIM) * G_KEY_DIM ** -0.5
    gkh = gk.astype(F32).reshape(B, T, G_HEADS, G_KEY_DIM)
    gvh = gv.astype(F32).reshape(B, T, G_HEADS, G_VAL_DIM)
    z = jnp.einsum('btnr,nrk->btnk', glr.astype(F32).reshape(B, T, 2, G_DECAY_RANK), g_decay_w2.astype(F32)) + g_decay_b.astype(F32)
    la = (jax.nn.log_sigmoid(z) / G_DECAY_TAU).reshape(B, T, 2, G_HEADS, G_KEY_DIM)
    og = gla_bidirectional(_both(gqh, gqh), _both(gkh, gkh), _both(gvh, gvh), _both(la[:, :, 0], la[:, :, 1]))
    og = og[0] + jnp.flip(og[1], axis=1)
    og = head_rms(og, g_norm) * jax.nn.silu(gg.astype(F32))
    merged = jnp.concatenate([hm, og], axis=-1).astype(h.dtype)
    return merged @ w_out


def rwkv7_scan_bidirectional(r, w_f, w_b, k, v, a, b):
    B, T, H, N = r.shape
    tm = lambda t: jnp.moveaxis(t, 1, 0)

    def step(S, inp):
        r_t, w_t, k_t, v_t, a_t, b_t = inp
        sa = jnp.einsum('bhvk,bhk->bhv', S, a_t)
        S = S * w_t[..., None, :] + sa[..., :, None] * b_t[..., None, :] + v_t[..., :, None] * k_t[..., None, :]
        return S, jnp.einsum('bhvk,bhk->bhv', S, r_t)

    S0 = jnp.zeros((B, H, N, N), F32)
    rt, kt, vt, at, bt = tm(r), tm(k), tm(v), tm(a), tm(b)
    _, y_f = lax.scan(step, S0, (rt, tm(w_f), kt, vt, at, bt))
    _, y_b = lax.scan(step, S0, (rt, tm(w_b), kt, vt, at, bt), reverse=True)
    return jnp.moveaxis(y_f + y_b, 0, 1)


def rwkv7_mixer(h, mu, w_rkv, w0, w1, w2, a0, a1, a2, g1, g2, k_k, k_a, r_k, ln_w, ln_b, w_o):
    B, T, C = h.shape
    zero = jnp.zeros_like(h[:, :1])
    h_prev = jnp.concatenate([zero, h[:, :-1]], axis=1)
    h_next = jnp.concatenate([h[:, 1:], zero], axis=1)
    hh = 0.5 * (h_prev + h_next) - h
    xr, xw, xk, xv, xa, xg = (h + hh * mu[i] for i in range(6))
    r = xr @ w_rkv[0]
    k = xk @ w_rkv[1]
    v = xv @ w_rkv[2]
    a = jax.nn.sigmoid(a0 + (xa @ a1) @ a2)
    g = jax.nn.sigmoid(xg @ g1) @ g2

    def decay(d):
        z = (w0[d] + jnp.tanh(xw @ w1[d]) @ w2[d]).astype(F32)
        return jnp.exp(-jnp.exp(-jax.nn.softplus(-z) - 0.5))

    def heads(t):
        return t.astype(F32).reshape(B, T, R_HEADS, R_HEAD_DIM)

    kk = heads(k * k_k)
    kk = kk / jnp.maximum(jnp.sqrt(jnp.sum(kk * kk, axis=-1, keepdims=True)), 1e-12)
    k = k * (1 + (a - 1) * k_a)
    rh, kh, vh, ah = heads(r), heads(k), heads(v), heads(a)
    y = rwkv7_scan_bidirectional(rh, heads(decay(0)), heads(decay(1)), kh, vh, -kk, kk * ah)
    mean = jnp.mean(y, axis=-1, keepdims=True)
    var = jnp.mean(jnp.square(y - mean), axis=-1, keepdims=True)
    y = ((y - mean) * lax.rsqrt(var + R_GN_EPS)).reshape(B, T, C) * ln_w.astype(F32) + ln_b.astype(F32)
    bonus = jnp.sum(rh * kh * r_k.astype(F32), axis=-1, keepdims=True) * vh
    y = (y + bonus.reshape(B, T, C)) * g.astype(F32)
    return y.astype(h.dtype) @ w_o


def memory_cross_attention(h, mem_n, wq, wkv, wo):
    B, T, _ = h.shape
    M = mem_n.shape[1]
    q = (h @ wq).reshape(B, T, X_HEADS, X_HEAD_DIM)
    k, v = jnp.split(mem_n @ wkv, 2, axis=-1)
    k = k.reshape(B, M, X_HEADS, X_HEAD_DIM)
    v = v.reshape(B, M, X_HEADS, X_HEAD_DIM)
    s = jnp.einsum('bthd,bmhd->bhtm', q, k).astype(F32) * X_HEAD_DIM ** -0.5
    p = jax.nn.softmax(s, axis=-1).astype(v.dtype)
    o = jnp.einsum('bhtm,bmhd->bthd', p, v).reshape(B, T, D_MODEL)
    return o @ wo


def swiglu_ffn(h, w_gate, w_up, w_down):
    return (jax.nn.silu(h @ w_gate) * (h @ w_up)) @ w_down


def setup_inputs(seed: int = 0) -> dict:
    key = jax.random.key(seed)
    keys = iter(jax.random.split(key, 64))

    def nrm(shape, scale):
        return jax.random.normal(next(keys), shape, jnp.float32) * scale

    def gain(shape):
        return 1.0 + nrm(shape, 0.02)

    D = D_MODEL
    NE = (DEPTH + 1) // 2
    NO = DEPTH // 2
    return {
        'x': nrm((BATCH, SEQ, D), 1.0),
        'mem': nrm((BATCH, N_MEM, D), 1.0),
        'norm_mix': gain((DEPTH, D)),
        'norm_xattn': gain((DEPTH, D)),
        'norm_mem': gain((DEPTH, D)),
        'norm_ffn': gain((DEPTH, D)),
        'norm_final': gain((D,)),
        'xa_wq': nrm((DEPTH, D, D), D ** -0.5),
        'xa_wkv': nrm((DEPTH, D, 2 * D), D ** -0.5),
        'xa_wo': nrm((DEPTH, D, D), D ** -0.5),
        'ffn_w_gate': nrm((DEPTH, D, D_FF), D ** -0.5),
        'ffn_w_up': nrm((DEPTH, D, D_FF), D ** -0.5),
        'ffn_w_down': nrm((DEPTH, D_FF, D), D_FF ** -0.5),
        'ev_w_in': nrm((NE, D, EVEN_IN_WIDTH), D ** -0.5),
        'ev_conv_qk': nrm((NE, CONV_WIDTH, 2 * M_WIDTH), CONV_WIDTH ** -0.5),
        'ev_m_ig_bias': nrm((NE, 2, M_HEADS), 0.1),
        'ev_m_fg_bias': jnp.linspace(3.0, 6.0, M_HEADS, dtype=jnp.float32) + nrm((NE, 2, M_HEADS), 0.1),
        'ev_m_norm': gain((NE, M_WIDTH)),
        'ev_g_decay_w2': nrm((NE, 2, G_DECAY_RANK, G_KEY_WIDTH), G_DECAY_RANK ** -0.5),
        'ev_g_decay_b': nrm((NE, 2, G_KEY_WIDTH), 0.1),
        'ev_g_norm': gain((NE, G_VAL_WIDTH)),
        'ev_w_out': nrm((NE, EVEN_OUT_WIDTH, D), EVEN_OUT_WIDTH ** -0.5),
        'od_mu': jax.random.uniform(next(keys), (NO, 6, D), jnp.float32),
        'od_w_rkv': nrm((NO, 3, D, D), D ** -0.5),
        'od_w0': jnp.linspace(-6.0, -1.0, D, dtype=jnp.float32) + nrm((NO, 2, D), 0.1),
        'od_w1': nrm((NO, 2, D, R_DECAY_RANK), D ** -0.5),
        'od_w2': nrm((NO, 2, R_DECAY_RANK, D), 0.5 * R_DECAY_RANK ** -0.5),
        'od_a0': nrm((NO, D), 0.1),
        'od_a1': nrm((NO, D, R_A_RANK), D ** -0.5),
        'od_a2': nrm((NO, R_A_RANK, D), R_A_RANK ** -0.5),
        'od_g1': nrm((NO, D, R_GATE_RANK), D ** -0.5),
        'od_g2': nrm((NO, R_GATE_RANK, D), R_GATE_RANK ** -0.5),
        'od_k_k': 0.85 + nrm((NO, D), 0.02),
        'od_k_a': 1.0 + nrm((NO, D), 0.02),
        'od_r_k': -0.04 + nrm((NO, R_HEADS, R_HEAD_DIM), 0.02),
        'od_ln_w': gain((NO, D)),
        'od_ln_b': nrm((NO, D), 0.02),
        'od_w_o': nrm((NO, D, D), D ** -0.5),
    }


def reference(x, mem, norm_mix, norm_xattn, norm_mem, norm_ffn, norm_final,
              xa_wq, xa_wkv, xa_wo, ffn_w_gate, ffn_w_up, ffn_w_down,
              ev_w_in, ev_conv_qk, ev_m_ig_bias, ev_m_fg_bias, ev_m_norm,
              ev_g_decay_w2, ev_g_decay_b, ev_g_norm, ev_w_out,
              od_mu, od_w_rkv, od_w0, od_w1, od_w2, od_a0, od_a1, od_a2,
              od_g1, od_g2, od_k_k, od_k_a, od_r_k, od_ln_w, od_ln_b, od_w_o):
    for layer in range(DEPTH):
        h = rms_norm(x, norm_mix[layer])
        if layer % 2 == 0:
            e = layer // 2
            mix = even_mixer(h, ev_w_in[e], ev_conv_qk[e], ev_m_ig_bias[e], ev_m_fg_bias[e], ev_m_norm[e],
                             ev_g_decay_w2[e], ev_g_decay_b[e], ev_g_norm[e], ev_w_out[e])
        else:
            o = layer // 2
            mix = rwkv7_mixer(h, od_mu[o], od_w_rkv[o], od_w0[o], od_w1[o], od_w2[o], od_a0[o], od_a1[o], od_a2[o],
                              od_g1[o], od_g2[o], od_k_k[o], od_k_a[o], od_r_k[o], od_ln_w[o], od_ln_b[o], od_w_o[o])
        x = x + mix
        x = x + memory_cross_attention(rms_norm(x, norm_xattn[layer]), rms_norm(mem, norm_mem[layer]),
                                       xa_wq[layer], xa_wkv[layer], xa_wo[layer])
        x = x + swiglu_ffn(rms_norm(x, norm_ffn[layer]), ffn_w_gate[layer], ffn_w_up[layer], ffn_w_down[layer])
    return rms_norm(x, norm_final)
```

```python
import functools

import jax
import jax.numpy as jnp
from jax import lax
from jax.experimental import pallas as pl
from jax.experimental.pallas import tpu as pltpu

F32 = jnp.float32
MXU_DTYPE = jnp.bfloat16
HIGHEST = lax.Precision.HIGHEST
LANES = 128
SUBLANES = 8
VMEM_LIMIT_BYTES = 60 * 1024 * 1024
NEG_BIG = -1e30

EPS = 1e-6
CONV_WIDTH = 3
M_HEADS = 4
M_HEAD_DIM = 128
G_HEADS = 4
G_KEY_DIM = 64
G_VAL_DIM = 128
G_DECAY_RANK = 16
G_DECAY_TAU = 16.0
R_HEAD_DIM = 64
R_GN_EPS = 64e-5
X_HEADS = 4

MLSTM_CHUNK = 256
GLA_CHUNK = 64
RWKV_CHUNK = 64

_EV_MQ, _EV_MK, _EV_MV, _EV_MO = 0, 4, 8, 12
_EV_GQ, _EV_GK, _EV_GV, _EV_GG, _EV_GATE = 16, 18, 20, 24, 28
_EV_BLOCKS = 29
_GATE_IG, _GATE_FG, _GATE_LR = 0, 8, 16


def _mx(a):
    return a.astype(MXU_DTYPE)


def _dot(a, b):
    return jnp.dot(_mx(a), _mx(b), preferred_element_type=F32)


def _dot_nt(a, b):
    return lax.dot_general(_mx(a), _mx(b), (((1,), (1,)), ((), ())), preferred_element_type=F32)


def _dot_tn(a, b):
    return lax.dot_general(_mx(a), _mx(b), (((0,), (0,)), ((), ())), preferred_element_type=F32)


def _dot_f32(a, b):
    return jnp.dot(a, b, precision=HIGHEST, preferred_element_type=F32)


def _rms(x, g):
    return x * lax.rsqrt(jnp.mean(x * x, axis=-1, keepdims=True) + EPS) * g


def _log_sigmoid(x):
    return jnp.minimum(x, 0.0) - jnp.log1p(jnp.exp(-jnp.abs(x)))


def _silu(x):
    return x * jax.nn.sigmoid(x)


def _params(*sem):
    return pltpu.CompilerParams(dimension_semantics=sem, vmem_limit_bytes=VMEM_LIMIT_BYTES)


def _resident(shape):
    nd = len(shape)
    return pl.BlockSpec(shape, lambda *_: (0,) * nd, pipeline_mode=pl.Buffered(1))


def _scan_masks(n, reverse):
    row = lax.broadcasted_iota(jnp.int32, (n, n), 0)
    col = lax.broadcasted_iota(jnp.int32, (n, n), 1)
    if reverse:
        return col >= row, col > row
    return col <= row, col < row


def _rms_matmul_kernel(x_ref, g_ref, w_ref, o_ref):
    h = _rms(x_ref[...], g_ref[...])
    o_ref[...] = _dot(h, w_ref[...]).astype(o_ref.dtype)


def rms_matmul(x2d, g, w, *, tm, out_dtype):
    n, d = x2d.shape
    nout = w.shape[1]
    return pl.pallas_call(
        _rms_matmul_kernel,
        grid=(n // tm,),
        in_specs=[pl.BlockSpec((tm, d), lambda i: (i, 0)), _resident((1, d)), _resident((d, nout))],
        out_specs=pl.BlockSpec((tm, nout), lambda i: (i, 0)),
        out_shape=jax.ShapeDtypeStruct((n, nout), out_dtype),
        compiler_params=_params("parallel"),
        name="rms_matmul",
    )(x2d, g.reshape(1, d), w)


def _extract_col(x, lane_idx):
    lane = lax.broadcasted_iota(jnp.int32, x.shape, 1)
    return jnp.sum(jnp.where(lane == lane_idx, x, 0.0), axis=1, keepdims=True)


def _extract_row(x, row_idx):
    row = lax.broadcasted_iota(jnp.int32, x.shape, 0)
    return jnp.sum(jnp.where(row == row_idx, x, 0.0), axis=0, keepdims=True)


def _mlstm_kernel(q_ref, k_ref, v_ref, o_ref, gate_ref, conv_ref, gbias_ref, norm_ref, out_ref,
                  qc_ref, kc_ref, gl_ref, hf_ref, hb_ref, c_ref, m_ref, *, seq, chunk):
    L = chunk
    nc = seq // L
    d = M_HEAD_DIM
    head = pl.program_id(1)

    row_id = lax.broadcasted_iota(jnp.int32, (L, d), 0)
    lane_id = lax.broadcasted_iota(jnp.int32, (L, LANES), 1)
    is_fg = (lane_id >= _GATE_FG) & (lane_id < _GATE_LR)

    def conv_silu(src_ref, w, r0, c):
        x = src_ref[0, pl.ds(r0, L), :]
        prev_row = src_ref[0, pl.ds(jnp.maximum(r0 - 1, 0), 1), :] * (c > 0).astype(F32)
        next_row = src_ref[0, pl.ds(jnp.minimum(r0 + L, seq - 1), 1), :] * (c < nc - 1).astype(F32)
        x_prev = jnp.where(row_id == 0, prev_row, pltpu.roll(x, 1, axis=0))
        x_next = jnp.where(row_id == L - 1, next_row, pltpu.roll(x, L - 1, axis=0))
        return _silu(x_prev * w[0:1, :] + x * w[1:2, :] + x_next * w[2:3, :])

    def prologue(c, carry):
        r0 = pl.multiple_of(c * L, L)
        qc_ref[pl.ds(r0, L), :] = (conv_silu(q_ref, conv_ref[0], r0, c) * (d ** -0.5)).astype(qc_ref.dtype)
        kc_ref[pl.ds(r0, L), :] = conv_silu(k_ref, conv_ref[1], r0, c)
        g = gate_ref[0, pl.ds(r0, L), :] + gbias_ref[...]
        gl_ref[pl.ds(r0, L), :] = jnp.where(is_fg, _log_sigmoid(g), g)
        return carry

    lax.fori_loop(0, nc, prologue, 0)

    c_ref[...] = jnp.zeros_like(c_ref)
    m_ref[...] = jnp.zeros_like(m_ref)
    ones = jnp.ones((L, d), MXU_DTYPE)

    def chunk_step(direction, c, dst_ref):
        reverse = direction == 1
        r0 = pl.multiple_of(c * L, L)
        incl, _ = _scan_masks(L, reverse)
        ig_lane = _GATE_IG + direction * M_HEADS + head
        fg_lane = _GATE_FG + direction * M_HEADS + head
        g = gl_ref[pl.ds(r0, L), :]
        bcum = _dot_f32(incl.astype(F32), g)
        ig_col = _extract_col(g, ig_lane)
        b_col = _extract_col(bcum, fg_lane)
        ig_row = _extract_row(g.T, ig_lane)
        b_row = _extract_row(bcum.T, fg_lane)
        g_tot = b_col[0:1, :] if reverse else b_col[L - 1:L, :]
        m_prev = m_ref[direction][:, 0:1]

        dmat = jnp.where(incl, b_col - b_row + ig_row, NEG_BIG)
        inter_log = b_col + m_prev
        m_t = jnp.maximum(inter_log, jnp.max(dmat, axis=1, keepdims=True))
        q = qc_ref[pl.ds(r0, L), :]
        k = kc_ref[pl.ds(r0, L), :]
        v_aug = jnp.concatenate([_mx(v_ref[0, pl.ds(r0, L), :]), ones], axis=1)
        sc = _dot_nt(q, k) * jnp.exp(dmat - m_t)
        w_inter = jnp.exp(inter_log - m_t)
        c_aug = c_ref[direction]
        numden = _dot(sc, v_aug) + w_inter * _dot(q, c_aug)
        num = numden[:, :d]
        den = numden[:, d:]
        dst_ref[pl.ds(r0, L), :] = num / jnp.maximum(jnp.abs(den), jnp.exp(-m_t))

        kw_log = g_tot - b_col + ig_col
        m_new = jnp.maximum(g_tot + m_prev, jnp.max(kw_log, axis=0, keepdims=True))
        kw = jnp.exp(kw_log - m_new)
        decay = jnp.exp(g_tot + m_prev - m_new)
        c_ref[direction] = decay * c_aug + _dot_tn(k * kw, v_aug)
        m_ref[direction] = jnp.broadcast_to(m_new, (1, LANES))

    def body(c, carry):
        chunk_step(0, c, hf_ref)
        chunk_step(1, nc - 1 - c, hb_ref)
        return carry

    lax.fori_loop(0, nc, body, 0)

    def epilogue(c, carry):
        r0 = pl.multiple_of(c * L, L)
        h = hf_ref[pl.ds(r0, L), :] + hb_ref[pl.ds(r0, L), :]
        h = _rms(h, norm_ref[...])
        out_ref[0, pl.ds(r0, L), :] = h * jax.nn.sigmoid(o_ref[0, pl.ds(r0, L), :])
        return carry

    lax.fori_loop(0, nc, epilogue, 0)


def mlstm_mixer(p, conv_qk, gate_bias, m_norm, *, chunk):
    b, t, _ = p.shape
    d = M_HEAD_DIM
    blk = lambda off: pl.BlockSpec((1, t, d), lambda i, h, off=off: (i, 0, off + h))
    conv = conv_qk.reshape(CONV_WIDTH, 2, M_HEADS, d).transpose(2, 1, 0, 3)
    return pl.pallas_call(
        functools.partial(_mlstm_kernel, seq=t, chunk=chunk),
        grid=(b, M_HEADS),
        in_specs=[blk(_EV_MQ), blk(_EV_MK), blk(_EV_MV), blk(_EV_MO),
                  pl.BlockSpec((1, t, LANES), lambda i, h: (i, 0, _EV_GATE)),
                  pl.BlockSpec((None, 2, CONV_WIDTH, d), lambda i, h: (h, 0, 0, 0)),
                  pl.BlockSpec((1, LANES), lambda i, h: (0, 0)),
                  pl.BlockSpec((1, d), lambda i, h: (0, h))],
        out_specs=pl.BlockSpec((1, t, d), lambda i, h: (i, 0, h)),
        out_shape=jax.ShapeDtypeStruct((b, t, M_HEADS * d), F32),
        scratch_shapes=[pltpu.VMEM((t, d), MXU_DTYPE),
                        pltpu.VMEM((t, d), F32),
                        pltpu.VMEM((t, LANES), F32),
                        pltpu.VMEM((t, d), F32),
                        pltpu.VMEM((t, d), F32),
                        pltpu.VMEM((2, d, 2 * d), F32),
                        pltpu.VMEM((2, 1, LANES), F32)],
        compiler_params=_params("parallel", "parallel"),
        name="mlstm",
    )(p, p, p, p, p, conv, gate_bias, m_norm.reshape(1, M_HEADS * d))


def _gla_kernel(q_ref, k_ref, v0_ref, v1_ref, g0_ref, g1_ref, gate_ref, w2_ref, b2_ref, norm_ref, out_ref,
                of_ref, ob_ref, s_ref, *, seq, chunk):
    L = chunk
    nc = seq // L
    dk2 = 2 * G_KEY_DIM
    dv = G_VAL_DIM
    lane_k = lax.broadcasted_iota(jnp.int32, (L, dk2), 1)
    head0 = (lane_k < G_KEY_DIM).astype(F32)
    head1 = 1.0 - head0
    s_row = lax.broadcasted_iota(jnp.int32, (2 * dv, dk2), 0)
    s_col = lax.broadcasted_iota(jnp.int32, (2 * dv, dk2), 1)
    s_mask = ((s_row < dv) == (s_col < G_KEY_DIM)).astype(F32)

    s_ref[...] = jnp.zeros_like(s_ref)

    def chunk_step(direction, c, dst_ref):
        reverse = direction == 1
        r0 = pl.multiple_of(c * L, L)
        incl, _ = _scan_masks(L, reverse)
        z = _dot(gate_ref[0, pl.ds(r0, L), :], w2_ref[direction, 0]) + b2_ref[direction, 0]
        la = _log_sigmoid(z) * (1.0 / G_DECAY_TAU)
        bcum = _dot_f32(incl.astype(F32), la)
        g_tot = bcum[0:1, :] if reverse else bcum[L - 1:L, :]
        mid = bcum[L // 2:L // 2 + 1, :]
        q = q_ref[0, pl.ds(r0, L), :] * (G_KEY_DIM ** -0.5)
        k = k_ref[0, pl.ds(r0, L), :]
        v = jnp.concatenate([_mx(v0_ref[0, pl.ds(r0, L), :]), _mx(v1_ref[0, pl.ds(r0, L), :])], axis=1)
        q_in = q * jnp.exp(bcum - mid)
        k_in = _mx(k * jnp.exp(mid - bcum))
        a0 = jnp.where(incl, _dot_nt(q_in * head0, k_in), 0.0)
        a1 = jnp.where(incl, _dot_nt(q_in * head1, k_in), 0.0)
        o_intra = jnp.concatenate([_dot(a0, v[:, :dv]), _dot(a1, v[:, dv:])], axis=1)
        s_t = s_ref[direction]
        dst_ref[pl.ds(r0, L), :] = o_intra + _dot_nt(q * jnp.exp(bcum), s_t)
        s_new = s_t * jnp.exp(g_tot) + _dot_tn(v, k * jnp.exp(g_tot - bcum))
        s_ref[direction] = s_new * s_mask

    def body(c, carry):
        chunk_step(0, c, of_ref)
        chunk_step(1, nc - 1 - c, ob_ref)
        return carry

    lax.fori_loop(0, nc, body, 0)

    def epilogue(c, carry):
        r0 = pl.multiple_of(c * L, L)
        o = of_ref[pl.ds(r0, L), :] + ob_ref[pl.ds(r0, L), :]
        g = jnp.concatenate([g0_ref[0, pl.ds(r0, L), :], g1_ref[0, pl.ds(r0, L), :]], axis=1)
        nrm = norm_ref[...]
        o = jnp.concatenate([_rms(o[:, :dv], nrm[:, :dv]), _rms(o[:, dv:], nrm[:, dv:])], axis=1)
        out_ref[0, pl.ds(r0, L), :] = o * _silu(g)
        return carry

    lax.fori_loop(0, nc, epilogue, 0)


def gla_mixer(p, decay_w2, decay_b, g_norm, *, chunk):
    b, t, _ = p.shape
    dv = G_VAL_DIM
    pairs = G_HEADS // 2
    dk2 = 2 * G_KEY_DIM
    w2 = jnp.zeros((2, LANES, pairs * dk2), F32)
    for direction in range(2):
        lo = _GATE_LR + direction * G_DECAY_RANK
        w2 = w2.at[direction, lo:lo + G_DECAY_RANK].set(decay_w2[direction])
    w2 = _mx(w2.reshape(2, LANES, pairs, dk2).transpose(0, 2, 1, 3))
    b2 = decay_b.reshape(2, pairs, 1, dk2)
    blk = lambda off, mult: pl.BlockSpec((1, t, LANES), lambda i, h, off=off, mult=mult: (i, 0, off + mult * h))
    return pl.pallas_call(
        functools.partial(_gla_kernel, seq=t, chunk=chunk),
        grid=(b, pairs),
        in_specs=[blk(_EV_GQ, 1), blk(_EV_GK, 1), blk(_EV_GV, 2), blk(_EV_GV + 1, 2),
                  blk(_EV_GG, 2), blk(_EV_GG + 1, 2),
                  pl.BlockSpec((1, t, LANES), lambda i, h: (i, 0, _EV_GATE)),
                  pl.BlockSpec((2, 1, LANES, dk2), lambda i, h: (0, h, 0, 0)),
                  pl.BlockSpec((2, 1, 1, dk2), lambda i, h: (0, h, 0, 0)),
                  pl.BlockSpec((1, 2 * dv), lambda i, h: (0, h))],
        out_specs=pl.BlockSpec((1, t, 2 * dv), lambda i, h: (i, 0, h)),
        out_shape=jax.ShapeDtypeStruct((b, t, G_HEADS * dv), F32),
        scratch_shapes=[pltpu.VMEM((t, 2 * dv), F32),
                        pltpu.VMEM((t, 2 * dv), F32),
                        pltpu.VMEM((2, 2 * dv, dk2), F32)],
        compiler_params=_params("parallel", "parallel"),
        name="gla",
    )(p, p, p, p, p, p, p, w2, b2, g_norm.reshape(1, G_HEADS * dv))


def _even_out_kernel(x_ref, hm_ref, og_ref, wm_ref, wg_ref, o_ref):
    o_ref[...] = x_ref[...] + _dot(hm_ref[...], wm_ref[...]) + _dot(og_ref[...], wg_ref[...])


def even_out_proj(x2d, hm, og, w_out, *, tm):
    n, d = x2d.shape
    wm, wg = _mx(w_out[:hm.shape[1]]), _mx(w_out[hm.shape[1]:])
    row = lambda width: pl.BlockSpec((tm, width), lambda i: (i, 0))
    return pl.pallas_call(
        _even_out_kernel,
        grid=(n // tm,),
        in_specs=[row(d), row(hm.shape[1]), row(og.shape[1]), _resident(wm.shape), _resident(wg.shape)],
        out_specs=row(d),
        out_shape=jax.ShapeDtypeStruct((n, d), F32),
        compiler_params=_params("parallel"),
        name="even_out_proj",
    )(x2d, hm, og, wm, wg)


def _gated_out_kernel(x_ref, y_ref, g_ref, w_ref, o_ref):
    o_ref[...] = x_ref[...] + _dot(y_ref[...] * g_ref[...], w_ref[...])


def gated_out_proj(x2d, y, g, w, *, tm):
    n, d = x2d.shape
    row = pl.BlockSpec((tm, d), lambda i: (i, 0))
    return pl.pallas_call(
        _gated_out_kernel,
        grid=(n // tm,),
        in_specs=[row, row, row, _resident(w.shape)],
        out_specs=row,
        out_shape=jax.ShapeDtypeStruct((n, d), F32),
        compiler_params=_params("parallel"),
        name="gated_out_proj",
    )(x2d, y, g, _mx(w))


def _rwkv_in_kernel(x_ref, xp_ref, xn_ref, nrm_ref, mu_ref, wrkv_ref, w1_ref, w2_ref, w0_ref,
                    a1_ref, a2_ref, a0_ref, g1_ref, g2_ref,
                    r_ref, k_ref, v_ref, a_ref, lwf_ref, lwb_ref, g_ref, pad_ref, *, tm):
    i = pl.program_id(1)
    nt = pl.num_programs(1)
    nrm = nrm_ref[...]
    h = _rms(x_ref[0], nrm)
    first = (i > 0).astype(F32)
    last = (i < nt - 1).astype(F32)
    pad_ref[pl.ds(0, SUBLANES), :] = _rms(xp_ref[0, 0], nrm) * first
    pad_ref[pl.ds(SUBLANES, tm), :] = h
    pad_ref[pl.ds(SUBLANES + tm, SUBLANES), :] = _rms(xn_ref[0, 0], nrm) * last
    h_prev = pad_ref[pl.ds(SUBLANES - 1, tm), :]
    h_next = pad_ref[pl.ds(SUBLANES + 1, tm), :]
    hh = 0.5 * (h_prev + h_next) - h
    mix = lambda j: h + hh * mu_ref[j:j + 1, :]
    xr, xw, xk, xv, xa, xg = (mix(j) for j in range(6))
    r_ref[0] = _dot(xr, wrkv_ref[0])
    k_ref[0] = _dot(xk, wrkv_ref[1])
    v_ref[0] = _dot(xv, wrkv_ref[2])
    a_ref[0] = jax.nn.sigmoid(a0_ref[...] + _dot(_dot(xa, a1_ref[...]), a2_ref[...]))
    g_ref[0] = _dot(jax.nn.sigmoid(_dot(xg, g1_ref[...])), g2_ref[...])
    for direction, dst in ((0, lwf_ref), (1, lwb_ref)):
        z = w0_ref[direction:direction + 1, :] + _dot(jnp.tanh(_dot(xw, w1_ref[direction])), w2_ref[direction])
        dst[0] = -jnp.exp(_log_sigmoid(z) - 0.5)


def rwkv_in(x, norm, mu, w_rkv, w0, w1, w2, a0, a1, a2, g1, g2, *, tm):
    b, t, d = x.shape
    x8 = x.reshape(b, t // SUBLANES, SUBLANES, d)
    tb = tm // SUBLANES
    nb = t // SUBLANES
    rank_g = g1.shape[1]
    pad_g = -rank_g % LANES
    g1p = jnp.pad(g1, ((0, 0), (0, pad_g)))
    g2p = jnp.pad(g2, ((0, pad_g), (0, 0)))
    tile = pl.BlockSpec((1, tm, d), lambda i, j: (i, j, 0))
    out = jax.ShapeDtypeStruct((b, t, d), F32)
    weights = [norm.reshape(1, d), mu, _mx(w_rkv), _mx(w1), _mx(w2), w0,
               _mx(a1), _mx(a2), a0.reshape(1, d), _mx(g1p), _mx(g2p)]
    return pl.pallas_call(
        functools.partial(_rwkv_in_kernel, tm=tm),
        grid=(b, t // tm),
        in_specs=[tile,
                  pl.BlockSpec((1, 1, SUBLANES, d), lambda i, j: (i, jnp.maximum(j * tb - 1, 0), 0, 0)),
                  pl.BlockSpec((1, 1, SUBLANES, d), lambda i, j: (i, jnp.minimum((j + 1) * tb, nb - 1), 0, 0))]
                 + [_resident(w.shape) for w in weights],
        out_specs=[tile] * 7,
        out_shape=[out] * 7,
        scratch_shapes=[pltpu.VMEM((tm + 2 * SUBLANES, d), F32)],
        compiler_params=_params("parallel", "parallel"),
        name="rwkv_in",
    )(x, x8, x8, *weights)


def _unit_triangular_inverse(n_mat, block):
    size = n_mat.shape[0]
    eye = (lax.broadcasted_iota(jnp.int32, (size, size), 0)
           == lax.broadcasted_iota(jnp.int32, (size, size), 1)).astype(F32)
    acc = eye + n_mat
    power = n_mat
    span = 2
    while span < block:
        power = _dot_f32(power, power)
        acc = acc + _dot_f32(acc, power)
        span *= 2
    return acc


def _rwkv_scan_kernel(r_ref, k_ref, v_ref, a_ref, lwf_ref, lwb_ref, kk_ref, ka_ref, rk_ref, lnw_ref, lnb_ref,
                      out_ref, yf_ref, yb_ref, s_ref, *, seq, chunk):
    L = chunk
    nc = seq // L
    n = R_HEAD_DIM
    lane = lax.broadcasted_iota(jnp.int32, (L, 2 * n), 1)
    head0 = (lane < n).astype(F32)
    head1 = 1.0 - head0

    def head_sum(x):
        s0 = jnp.sum(x * head0, axis=1, keepdims=True)
        s1 = jnp.sum(x * head1, axis=1, keepdims=True)
        return s0 * head0 + s1 * head1

    def stack(x):
        return jnp.concatenate([x * head0, x * head1], axis=0)

    def load(c):
        r0 = pl.multiple_of(c * L, L)
        rows = pl.ds(r0, L)
        r, k, v, a = r_ref[0, rows, :], k_ref[0, rows, :], v_ref[0, rows, :], a_ref[0, rows, :]
        kk = k * kk_ref[...]
        kk = kk / jnp.maximum(jnp.sqrt(head_sum(kk * kk)), 1e-12)
        kp = k * (1.0 + (a - 1.0) * ka_ref[...])
        return rows, r, kp, v, -kk, kk * a

    s_ref[...] = jnp.zeros_like(s_ref)

    def chunk_step(direction, c, lw_ref, dst_ref):
        reverse = direction == 1
        rows, r, kp, v, a_s, b_s = load(c)
        lw = lw_ref[0, rows, :]
        incl_l, _ = _scan_masks(L, reverse)
        incl, strict = _scan_masks(2 * L, reverse)
        cin = _dot_f32(incl_l.astype(F32), lw)
        ctot = cin[0:1, :] if reverse else cin[L - 1:L, :]
        e_in = jnp.exp(cin)
        e_out = jnp.exp(-cin)
        e_end = jnp.exp(ctot - cin)
        x = jnp.concatenate([stack(a_s * jnp.exp(cin - lw)), stack(r * e_in)], axis=0)
        y = jnp.concatenate([stack(b_s * e_out), stack(kp * e_out)], axis=0)
        gram = _dot_nt(x, y)
        a_ab = jnp.where(strict, gram[:2 * L, :2 * L], 0.0)
        a_ak = jnp.where(strict, gram[:2 * L, 2 * L:], 0.0)
        a_rb = jnp.where(incl, gram[2 * L:, :2 * L], 0.0)
        a_rk = jnp.where(incl, gram[2 * L:, 2 * L:], 0.0)
        t_inv = _unit_triangular_inverse(a_ab, L)
        s_t = s_ref[direction]
        xs = _dot_nt(x, s_t)
        v2 = stack(v)
        u2 = _dot_f32(t_inv, _dot(a_ak, v2) + xs[:2 * L])
        uv = jnp.concatenate([u2, v2], axis=0)
        y2 = xs[2 * L:] + _dot(jnp.concatenate([a_rb, a_rk], axis=1), uv)
        dst_ref[rows, :] = y2[:L] + y2[L:]
        kb = jnp.concatenate([stack(b_s * e_end), stack(kp * e_end)], axis=0)
        s_ref[direction] = s_t * jnp.exp(ctot) + _dot_tn(uv, kb)

    def body(c, carry):
        chunk_step(0, c, lwf_ref, yf_ref)
        chunk_step(1, nc - 1 - c, lwb_ref, yb_ref)
        return carry

    lax.fori_loop(0, nc, body, 0)

    def epilogue(c, carry):
        rows, r, kp, v, _, _ = load(c)
        y = yf_ref[rows, :] + yb_ref[rows, :]
        mean = head_sum(y) * (1.0 / n)
        yc = y - mean
        var = head_sum(yc * yc) * (1.0 / n)
        yn = yc * lax.rsqrt(var + R_GN_EPS) * lnw_ref[...] + lnb_ref[...]
        out_ref[0, rows, :] = yn + head_sum(r * kp * rk_ref[...]) * v
        return carry

    lax.fori_loop(0, nc, epilogue, 0)


def rwkv_scan(r, k, v, a, lwf, lwb, k_k, k_a, r_k, ln_w, ln_b, *, chunk):
    b, t, d = r.shape
    w = 2 * R_HEAD_DIM
    pairs = d // w
    blk = pl.BlockSpec((1, t, w), lambda i, h: (i, 0, h))
    vec = pl.BlockSpec((1, w), lambda i, h: (0, h))
    row = lambda z: z.reshape(1, d)
    return pl.pallas_call(
        functools.partial(_rwkv_scan_kernel, seq=t, chunk=chunk),
        grid=(b, pairs),
        in_specs=[blk] * 6 + [vec] * 5,
        out_specs=blk,
        out_shape=jax.ShapeDtypeStruct((b, t, d), F32),
        scratch_shapes=[pltpu.VMEM((t, w), F32), pltpu.VMEM((t, w), F32), pltpu.VMEM((2, w, w), F32)],
        compiler_params=_params("parallel", "parallel"),
        name="rwkv_scan",
    )(r, k, v, a, lwf, lwb, row(k_k), row(k_a), row(r_k), row(ln_w), row(ln_b))


def _xattn_kernel(x_ref, nrm_ref, wq_ref, k_ref, v_ref, wo_ref, o_ref):
    x = x_ref[0]
    d = x.shape[-1]
    dh = d // X_HEADS
    q = _mx(_dot(_rms(x, nrm_ref[...]), wq_ref[...]))
    heads = []
    for hd in range(X_HEADS):
        cols = slice(hd * dh, (hd + 1) * dh)
        s = _dot_nt(q[:, cols], k_ref[0, :, cols]) * (dh ** -0.5)
        p = jnp.exp(s - jnp.max(s, axis=-1, keepdims=True))
        p = p / jnp.sum(p, axis=-1, keepdims=True)
        heads.append(_mx(_dot(p, v_ref[0, :, cols])))
    o_ref[0] = x + _dot(jnp.concatenate(heads, axis=1), wo_ref[...])


def cross_attention(x, kv, norm, wq, wo, *, tm):
    b, t, d = x.shape
    m = kv.shape[1]
    tile = pl.BlockSpec((1, tm, d), lambda i, j: (i, j, 0))
    return pl.pallas_call(
        _xattn_kernel,
        grid=(b, t // tm),
        in_specs=[tile, _resident((1, d)), _resident((d, d)),
                  pl.BlockSpec((1, m, d), lambda i, j: (i, 0, 0)),
                  pl.BlockSpec((1, m, d), lambda i, j: (i, 0, 1)),
                  _resident((d, d))],
        out_specs=tile,
        out_shape=jax.ShapeDtypeStruct((b, t, d), F32),
        compiler_params=_params("parallel", "parallel"),
        name="cross_attention",
    )(x, norm.reshape(1, d), _mx(wq), kv, kv, _mx(wo))


def _ffn_kernel(x_ref, nrm_ref, wg_ref, wu_ref, wd_ref, fin_ref, o_ref, *, final_norm):
    x = x_ref[...]
    h = _mx(_rms(x, nrm_ref[...]))
    act = _silu(_dot(h, wg_ref[...])) * _dot(h, wu_ref[...])
    y = x + _dot(act, wd_ref[...])
    o_ref[...] = _rms(y, fin_ref[...]) if final_norm else y


def swiglu_ffn(x2d, norm, w_gate, w_up, w_down, final_gain, *, tm, final_norm):
    n, d = x2d.shape
    dff = w_gate.shape[1]
    row = pl.BlockSpec((tm, d), lambda i: (i, 0))
    return pl.pallas_call(
        functools.partial(_ffn_kernel, final_norm=final_norm),
        grid=(n // tm,),
        in_specs=[row, _resident((1, d)), _resident((d, dff)), _resident((d, dff)), _resident((dff, d)),
                  _resident((1, d))],
        out_specs=row,
        out_shape=jax.ShapeDtypeStruct((n, d), F32),
        compiler_params=_params("parallel"),
        name="swiglu_ffn",
    )(x2d, norm.reshape(1, d), _mx(w_gate), _mx(w_up), _mx(w_down), final_gain.reshape(1, d))


def _pack_even_in(w_in):
    d = w_in.shape[0]
    mw, gkw, gvw = M_HEADS * M_HEAD_DIM, G_HEADS * G_KEY_DIM, G_HEADS * G_VAL_DIM
    widths = (mw, mw, mw, mw, 2 * M_HEADS, 2 * M_HEADS, gkw, gkw, gvw, gvw, 2 * G_DECAY_RANK)
    starts = [0]
    for wdt in widths:
        starts.append(starts[-1] + wdt)
    mq, mk, mv, mo, mi, mf, gq, gk, gv, gg, glr = (w_in[:, starts[j]:starts[j + 1]] for j in range(len(widths)))
    gate = jnp.concatenate([mi, mf, glr], axis=1)
    gate = jnp.pad(gate, ((0, 0), (0, LANES - gate.shape[1])))
    packed = jnp.concatenate([mq, mk, mv, mo, gq, gk, gv, gg, gate], axis=1)
    assert packed.shape == (d, _EV_BLOCKS * LANES)
    return _mx(packed)


def _even_mixer_layer(x, norm, w_in, conv_qk, ig_bias, fg_bias, m_norm, decay_w2, decay_b, g_norm, w_out, *, tm):
    b, t, d = x.shape
    x2d = x.reshape(b * t, d)
    p = rms_matmul(x2d, norm, _pack_even_in(w_in), tm=tm, out_dtype=F32).reshape(b, t, -1)
    gate_bias = jnp.concatenate([ig_bias.reshape(-1), fg_bias.reshape(-1)])
    gate_bias = jnp.pad(gate_bias, (0, LANES - gate_bias.shape[0])).reshape(1, LANES)
    hm = mlstm_mixer(p, conv_qk, gate_bias, m_norm, chunk=min(MLSTM_CHUNK, t))
    og = gla_mixer(p, decay_w2, decay_b, g_norm, chunk=min(GLA_CHUNK, t))
    x2d = even_out_proj(x2d, hm.reshape(b * t, -1), og.reshape(b * t, -1), w_out, tm=tm)
    return x2d.reshape(b, t, d)


def _rwkv_mixer_layer(x, norm, mu, w_rkv, w0, w1, w2, a0, a1, a2, g1, g2, k_k, k_a, r_k, ln_w, ln_b, w_o, *, tm):
    b, t, d = x.shape
    r, k, v, a, lwf, lwb, g = rwkv_in(x, norm, mu, w_rkv, w0, w1, w2, a0, a1, a2, g1, g2, tm=min(tm, 256))
    y = rwkv_scan(r, k, v, a, lwf, lwb, k_k, k_a, r_k.reshape(-1), ln_w, ln_b, chunk=min(RWKV_CHUNK, t))
    x2d = gated_out_proj(x.reshape(b * t, d), y.reshape(b * t, d), g.reshape(b * t, d), w_o, tm=tm)
    return x2d.reshape(b, t, d)


def kernel(x, mem, norm_mix, norm_xattn, norm_mem, norm_ffn, norm_final, xa_wq, xa_wkv, xa_wo, ffn_w_gate, ffn_w_up, ffn_w_down, ev_w_in, ev_conv_qk, ev_m_ig_bias, ev_m_fg_bias, ev_m_norm, ev_g_decay_w2, ev_g_decay_b, ev_g_norm, ev_w_out, od_mu, od_w_rkv, od_w0, od_w1, od_w2, od_a0, od_a1, od_a2, od_g1, od_g2, od_k_k, od_k_a, od_r_k, od_ln_w, od_ln_b, od_w_o):
    b, t, d = x.shape
    m = mem.shape[1]
    depth = norm_mix.shape[0]
    tm = min(512, t)
    for layer in range(depth):
        if layer % 2 == 0:
            e = layer // 2
            x = _even_mixer_layer(x, norm_mix[layer], ev_w_in[e], ev_conv_qk[e], ev_m_ig_bias[e], ev_m_fg_bias[e],
                                  ev_m_norm[e], ev_g_decay_w2[e], ev_g_decay_b[e], ev_g_norm[e], ev_w_out[e], tm=tm)
        else:
            o = layer // 2
            x = _rwkv_mixer_layer(x, norm_mix[layer], od_mu[o], od_w_rkv[o], od_w0[o], od_w1[o], od_w2[o], od_a0[o],
                                  od_a1[o], od_a2[o], od_g1[o], od_g2[o], od_k_k[o], od_k_a[o], od_r_k[o],
                                  od_ln_w[o], od_ln_b[o], od_w_o[o], tm=tm)
        kv = rms_matmul(mem.reshape(b * m, d), norm_mem[layer], _mx(xa_wkv[layer]), tm=min(512, b * m),
                        out_dtype=MXU_DTYPE).reshape(b, m, 2 * d)
        x = cross_attention(x, kv, norm_xattn[layer], xa_wq[layer], xa_wo[layer], tm=tm)
        last = layer == depth - 1
        x = swiglu_ffn(x.reshape(b * t, d), norm_ffn[layer], ffn_w_gate[layer], ffn_w_up[layer], ffn_w_down[layer],
                       norm_final, tm=tm, final_norm=last).reshape(b, t, d)
    return x
```

```python
import functools

import jax
import jax.numpy as jnp
from jax import lax
from jax.experimental import pallas as pl
from jax.experimental.pallas import tpu as pltpu

F32 = jnp.float32
MXU_DTYPE = jnp.bfloat16
LANES = 128
SUBLANES = 8
VMEM_LIMIT_BYTES = 60 * 1024 * 1024
NEG_BIG = -1e30

EPS = 1e-6
CONV_WIDTH = 3
M_HEADS = 4
M_HEAD_DIM = 128
G_HEADS = 4
G_KEY_DIM = 64
G_VAL_DIM = 128
G_DECAY_RANK = 16
G_DECAY_TAU = 16.0
R_HEAD_DIM = 64
R_GN_EPS = 64e-5
X_HEADS = 4

MLSTM_CHUNK = 256
GLA_CHUNK = 64
RWKV_CHUNK = 64
RWKV_BUILD_CHUNKS = 4

_EV_MQ, _EV_MK, _EV_MV, _EV_MO = 0, 4, 8, 12
_EV_GQ, _EV_GK, _EV_GV, _EV_GG, _EV_GATE = 16, 18, 20, 24, 28
_EV_BLOCKS = 29
_GATE_IG, _GATE_FG, _GATE_LR = 0, 8, 16


def _mx(a):
    return a.astype(MXU_DTYPE)


def _dot(a, b):
    return jnp.dot(_mx(a), _mx(b), preferred_element_type=F32)


def _dot_nt(a, b):
    return lax.dot_general(_mx(a), _mx(b), (((1,), (1,)), ((), ())), preferred_element_type=F32)


def _dot_tn(a, b):
    return lax.dot_general(_mx(a), _mx(b), (((0,), (0,)), ((), ())), preferred_element_type=F32)


def _scan_cumsum(mask, x):
    m = _mx(mask)
    hi = _mx(x)
    rest = x - hi.astype(F32)
    mid = _mx(rest)
    lo = _mx(rest - mid.astype(F32))
    dot = lambda t: jnp.dot(m, t, preferred_element_type=F32)
    return dot(hi) + dot(mid) + dot(lo)


def _rms(x, g):
    return x * lax.rsqrt(jnp.mean(x * x, axis=-1, keepdims=True) + EPS) * g


def _log_sigmoid(x):
    return jnp.minimum(x, 0.0) - jnp.log1p(jnp.exp(-jnp.abs(x)))


def _silu(x):
    return x * jax.nn.sigmoid(x)


def _params(*sem):
    return pltpu.CompilerParams(dimension_semantics=sem, vmem_limit_bytes=VMEM_LIMIT_BYTES)


def _resident(shape):
    nd = len(shape)
    return pl.BlockSpec(shape, lambda *_: (0,) * nd, pipeline_mode=pl.Buffered(1))


def _scan_masks(n, reverse):
    row = lax.broadcasted_iota(jnp.int32, (n, n), 0)
    col = lax.broadcasted_iota(jnp.int32, (n, n), 1)
    if reverse:
        return col >= row, col > row
    return col <= row, col < row


def _rms_matmul_kernel(x_ref, g_ref, w_ref, o_ref):
    h = _rms(x_ref[...], g_ref[...])
    o_ref[...] = _dot(h, w_ref[...]).astype(o_ref.dtype)


def rms_matmul(x2d, g, w, *, tm, out_dtype):
    n, d = x2d.shape
    nout = w.shape[1]
    return pl.pallas_call(
        _rms_matmul_kernel,
        grid=(n // tm,),
        in_specs=[pl.BlockSpec((tm, d), lambda i: (i, 0)), _resident((1, d)), _resident((d, nout))],
        out_specs=pl.BlockSpec((tm, nout), lambda i: (i, 0)),
        out_shape=jax.ShapeDtypeStruct((n, nout), out_dtype),
        compiler_params=_params("parallel"),
        name="rms_matmul",
    )(x2d, g.reshape(1, d), w)


def _extract_col(x, lane_idx):
    lane = lax.broadcasted_iota(jnp.int32, x.shape, 1)
    return jnp.sum(jnp.where(lane == lane_idx, x, 0.0), axis=1, keepdims=True)


def _extract_row(x, row_idx):
    row = lax.broadcasted_iota(jnp.int32, x.shape, 0)
    return jnp.sum(jnp.where(row == row_idx, x, 0.0), axis=0, keepdims=True)


def _mlstm_kernel(q_ref, k_ref, v_ref, o_ref, gate_ref, conv_ref, gbias_ref, norm_ref, out_ref,
                  qc_ref, kc_ref, gl_ref, hf_ref, hb_ref, c_ref, m_ref, *, seq, chunk):
    L = chunk
    nc = seq // L
    d = M_HEAD_DIM
    head = pl.program_id(1)

    row_id = lax.broadcasted_iota(jnp.int32, (L, d), 0)
    lane_id = lax.broadcasted_iota(jnp.int32, (L, LANES), 1)
    is_fg = (lane_id >= _GATE_FG) & (lane_id < _GATE_LR)

    def conv_silu(src_ref, w, r0, c):
        x = src_ref[0, pl.ds(r0, L), :]
        prev_row = src_ref[0, pl.ds(jnp.maximum(r0 - 1, 0), 1), :] * (c > 0).astype(F32)
        next_row = src_ref[0, pl.ds(jnp.minimum(r0 + L, seq - 1), 1), :] * (c < nc - 1).astype(F32)
        x_prev = jnp.where(row_id == 0, prev_row, pltpu.roll(x, 1, axis=0))
        x_next = jnp.where(row_id == L - 1, next_row, pltpu.roll(x, L - 1, axis=0))
        return _silu(x_prev * w[0:1, :] + x * w[1:2, :] + x_next * w[2:3, :])

    def prologue(c, carry):
        r0 = pl.multiple_of(c * L, L)
        qc_ref[pl.ds(r0, L), :] = (conv_silu(q_ref, conv_ref[0], r0, c) * (d ** -0.5)).astype(qc_ref.dtype)
        kc_ref[pl.ds(r0, L), :] = conv_silu(k_ref, conv_ref[1], r0, c)
        g = gate_ref[0, pl.ds(r0, L), :] + gbias_ref[...]
        gl_ref[pl.ds(r0, L), :] = jnp.where(is_fg, _log_sigmoid(g), g)
        return carry

    lax.fori_loop(0, nc, prologue, 0)

    c_ref[...] = jnp.zeros_like(c_ref)
    m_ref[...] = jnp.zeros_like(m_ref)
    ones = jnp.ones((L, d), MXU_DTYPE)

    def chunk_step(direction, c, dst_ref):
        reverse = direction == 1
        r0 = pl.multiple_of(c * L, L)
        incl, _ = _scan_masks(L, reverse)
        ig_lane = _GATE_IG + direction * M_HEADS + head
        fg_lane = _GATE_FG + direction * M_HEADS + head
        g = gl_ref[pl.ds(r0, L), :]
        bcum = _scan_cumsum(incl, g)
        ig_col = _extract_col(g, ig_lane)
        b_col = _extract_col(bcum, fg_lane)
        ig_row = _extract_row(g.T, ig_lane)
        b_row = _extract_row(bcum.T, fg_lane)
        g_tot = b_col[0:1, :] if reverse else b_col[L - 1:L, :]
        m_prev = m_ref[direction][:, 0:1]

        dmat = jnp.where(incl, b_col - b_row + ig_row, NEG_BIG)
        inter_log = b_col + m_prev
        m_t = jnp.maximum(inter_log, jnp.max(dmat, axis=1, keepdims=True))
        q = qc_ref[pl.ds(r0, L), :]
        k = kc_ref[pl.ds(r0, L), :]
        v_aug = jnp.concatenate([_mx(v_ref[0, pl.ds(r0, L), :]), ones], axis=1)
        sc = _dot_nt(q, k) * jnp.exp(dmat - m_t)
        w_inter = jnp.exp(inter_log - m_t)
        c_aug = c_ref[direction]
        numden = _dot(sc, v_aug) + w_inter * _dot(q, c_aug)
        num = numden[:, :d]
        den = numden[:, d:]
        dst_ref[pl.ds(r0, L), :] = num / jnp.maximum(jnp.abs(den), jnp.exp(-m_t))

        kw_log = g_tot - b_col + ig_col
        m_new = jnp.maximum(g_tot + m_prev, jnp.max(kw_log, axis=0, keepdims=True))
        kw = jnp.exp(kw_log - m_new)
        decay = jnp.exp(g_tot + m_prev - m_new)
        c_ref[direction] = decay * c_aug + _dot_tn(k * kw, v_aug)
        m_ref[direction] = jnp.broadcast_to(m_new, (1, LANES))

    def body(c, carry):
        chunk_step(0, c, hf_ref)
        chunk_step(1, nc - 1 - c, hb_ref)
        return carry

    lax.fori_loop(0, nc, body, 0)

    def epilogue(c, carry):
        r0 = pl.multiple_of(c * L, L)
        h = hf_ref[pl.ds(r0, L), :] + hb_ref[pl.ds(r0, L), :]
        h = _rms(h, norm_ref[...])
        out_ref[0, pl.ds(r0, L), :] = h * jax.nn.sigmoid(o_ref[0, pl.ds(r0, L), :])
        return carry

    lax.fori_loop(0, nc, epilogue, 0)


def mlstm_mixer(p, conv_qk, gate_bias, m_norm, *, chunk):
    b, t, _ = p.shape
    d = M_HEAD_DIM
    blk = lambda off: pl.BlockSpec((1, t, d), lambda i, h, off=off: (i, 0, off + h))
    conv = conv_qk.reshape(CONV_WIDTH, 2, M_HEADS, d).transpose(2, 1, 0, 3)
    return pl.pallas_call(
        functools.partial(_mlstm_kernel, seq=t, chunk=chunk),
        grid=(b, M_HEADS),
        in_specs=[blk(_EV_MQ), blk(_EV_MK), blk(_EV_MV), blk(_EV_MO),
                  pl.BlockSpec((1, t, LANES), lambda i, h: (i, 0, _EV_GATE)),
                  pl.BlockSpec((None, 2, CONV_WIDTH, d), lambda i, h: (h, 0, 0, 0)),
                  pl.BlockSpec((1, LANES), lambda i, h: (0, 0)),
                  pl.BlockSpec((1, d), lambda i, h: (0, h))],
        out_specs=pl.BlockSpec((1, t, d), lambda i, h: (i, 0, h)),
        out_shape=jax.ShapeDtypeStruct((b, t, M_HEADS * d), F32),
        scratch_shapes=[pltpu.VMEM((t, d), MXU_DTYPE),
                        pltpu.VMEM((t, d), F32),
                        pltpu.VMEM((t, LANES), F32),
                        pltpu.VMEM((t, d), F32),
                        pltpu.VMEM((t, d), F32),
                        pltpu.VMEM((2, d, 2 * d), F32),
                        pltpu.VMEM((2, 1, LANES), F32)],
        compiler_params=_params("parallel", "parallel"),
        name="mlstm",
    )(p, p, p, p, p, conv, gate_bias, m_norm.reshape(1, M_HEADS * d))


def _gla_kernel(q_ref, k_ref, v0_ref, v1_ref, g0_ref, g1_ref, gate_ref, w2_ref, b2_ref, norm_ref, out_ref,
                of_ref, ob_ref, s_ref, *, seq, chunk):
    L = chunk
    nc = seq // L
    dk2 = 2 * G_KEY_DIM
    dv = G_VAL_DIM
    lane_k = lax.broadcasted_iota(jnp.int32, (L, dk2), 1)
    head0 = (lane_k < G_KEY_DIM).astype(F32)
    head1 = 1.0 - head0
    s_row = lax.broadcasted_iota(jnp.int32, (2 * dv, dk2), 0)
    s_col = lax.broadcasted_iota(jnp.int32, (2 * dv, dk2), 1)
    s_mask = ((s_row < dv) == (s_col < G_KEY_DIM)).astype(F32)

    s_ref[...] = jnp.zeros_like(s_ref)

    def chunk_step(direction, c, dst_ref):
        reverse = direction == 1
        r0 = pl.multiple_of(c * L, L)
        incl, _ = _scan_masks(L, reverse)
        z = _dot(gate_ref[0, pl.ds(r0, L), :], w2_ref[direction, 0]) + b2_ref[direction, 0]
        la = _log_sigmoid(z) * (1.0 / G_DECAY_TAU)
        bcum = _scan_cumsum(incl, la)
        g_tot = bcum[0:1, :] if reverse else bcum[L - 1:L, :]
        mid = bcum[L // 2:L // 2 + 1, :]
        q = q_ref[0, pl.ds(r0, L), :] * (G_KEY_DIM ** -0.5)
        k = k_ref[0, pl.ds(r0, L), :]
        v = jnp.concatenate([_mx(v0_ref[0, pl.ds(r0, L), :]), _mx(v1_ref[0, pl.ds(r0, L), :])], axis=1)
        q_in = q * jnp.exp(bcum - mid)
        k_in = _mx(k * jnp.exp(mid - bcum))
        a0 = jnp.where(incl, _dot_nt(q_in * head0, k_in), 0.0)
        a1 = jnp.where(incl, _dot_nt(q_in * head1, k_in), 0.0)
        o_intra = jnp.concatenate([_dot(a0, v[:, :dv]), _dot(a1, v[:, dv:])], axis=1)
        s_t = s_ref[direction]
        dst_ref[pl.ds(r0, L), :] = o_intra + _dot_nt(q * jnp.exp(bcum), s_t)
        s_new = s_t * jnp.exp(g_tot) + _dot_tn(v, k * jnp.exp(g_tot - bcum))
        s_ref[direction] = s_new * s_mask

    def body(c, carry):
        chunk_step(0, c, of_ref)
        chunk_step(1, nc - 1 - c, ob_ref)
        return carry

    lax.fori_loop(0, nc, body, 0)

    def epilogue(c, carry):
        r0 = pl.multiple_of(c * L, L)
        o = of_ref[pl.ds(r0, L), :] + ob_ref[pl.ds(r0, L), :]
        g = jnp.concatenate([g0_ref[0, pl.ds(r0, L), :], g1_ref[0, pl.ds(r0, L), :]], axis=1)
        nrm = norm_ref[...]
        o = jnp.concatenate([_rms(o[:, :dv], nrm[:, :dv]), _rms(o[:, dv:], nrm[:, dv:])], axis=1)
        out_ref[0, pl.ds(r0, L), :] = o * _silu(g)
        return carry

    lax.fori_loop(0, nc, epilogue, 0)


def gla_mixer(p, decay_w2, decay_b, g_norm, *, chunk):
    b, t, _ = p.shape
    dv = G_VAL_DIM
    pairs = G_HEADS // 2
    dk2 = 2 * G_KEY_DIM
    w2 = jnp.zeros((2, LANES, pairs * dk2), F32)
    for direction in range(2):
        lo = _GATE_LR + direction * G_DECAY_RANK
        w2 = w2.at[direction, lo:lo + G_DECAY_RANK].set(decay_w2[direction])
    w2 = _mx(w2.reshape(2, LANES, pairs, dk2).transpose(0, 2, 1, 3))
    b2 = decay_b.reshape(2, pairs, 1, dk2)
    blk = lambda off, mult: pl.BlockSpec((1, t, LANES), lambda i, h, off=off, mult=mult: (i, 0, off + mult * h))
    return pl.pallas_call(
        functools.partial(_gla_kernel, seq=t, chunk=chunk),
        grid=(b, pairs),
        in_specs=[blk(_EV_GQ, 1), blk(_EV_GK, 1), blk(_EV_GV, 2), blk(_EV_GV + 1, 2),
                  blk(_EV_GG, 2), blk(_EV_GG + 1, 2),
                  pl.BlockSpec((1, t, LANES), lambda i, h: (i, 0, _EV_GATE)),
                  pl.BlockSpec((2, 1, LANES, dk2), lambda i, h: (0, h, 0, 0)),
                  pl.BlockSpec((2, 1, 1, dk2), lambda i, h: (0, h, 0, 0)),
                  pl.BlockSpec((1, 2 * dv), lambda i, h: (0, h))],
        out_specs=pl.BlockSpec((1, t, 2 * dv), lambda i, h: (i, 0, h)),
        out_shape=jax.ShapeDtypeStruct((b, t, G_HEADS * dv), F32),
        scratch_shapes=[pltpu.VMEM((t, 2 * dv), F32),
                        pltpu.VMEM((t, 2 * dv), F32),
                        pltpu.VMEM((2, 2 * dv, dk2), F32)],
        compiler_params=_params("parallel", "parallel"),
        name="gla",
    )(p, p, p, p, p, p, p, w2, b2, g_norm.reshape(1, G_HEADS * dv))


def _even_out_kernel(x_ref, hm_ref, og_ref, wm_ref, wg_ref, o_ref):
    o_ref[...] = x_ref[...] + _dot(hm_ref[...], wm_ref[...]) + _dot(og_ref[...], wg_ref[...])


def even_out_proj(x2d, hm, og, w_out, *, tm):
    n, d = x2d.shape
    wm, wg = _mx(w_out[:hm.shape[1]]), _mx(w_out[hm.shape[1]:])
    row = lambda width: pl.BlockSpec((tm, width), lambda i: (i, 0))
    return pl.pallas_call(
        _even_out_kernel,
        grid=(n // tm,),
        in_specs=[row(d), row(hm.shape[1]), row(og.shape[1]), _resident(wm.shape), _resident(wg.shape)],
        out_specs=row(d),
        out_shape=jax.ShapeDtypeStruct((n, d), F32),
        compiler_params=_params("parallel"),
        name="even_out_proj",
    )(x2d, hm, og, wm, wg)


def _gated_out_kernel(x_ref, y_ref, g_ref, w_ref, o_ref):
    o_ref[...] = x_ref[...] + _dot(y_ref[...] * g_ref[...], w_ref[...])


def gated_out_proj(x2d, y, g, w, *, tm):
    n, d = x2d.shape
    row = pl.BlockSpec((tm, d), lambda i: (i, 0))
    return pl.pallas_call(
        _gated_out_kernel,
        grid=(n // tm,),
        in_specs=[row, row, row, _resident(w.shape)],
        out_specs=row,
        out_shape=jax.ShapeDtypeStruct((n, d), F32),
        compiler_params=_params("parallel"),
        name="gated_out_proj",
    )(x2d, y, g, _mx(w))


def _rwkv_in_kernel(x_ref, xp_ref, xn_ref, nrm_ref, mu_ref, wrkv_ref, w1_ref, w2_ref, w0_ref,
                    a1_ref, a2_ref, a0_ref, g1_ref, g2_ref,
                    r_ref, k_ref, v_ref, a_ref, lwf_ref, lwb_ref, g_ref, pad_ref, *, tm):
    i = pl.program_id(1)
    nt = pl.num_programs(1)
    nrm = nrm_ref[...]
    h = _rms(x_ref[0], nrm)
    first = (i > 0).astype(F32)
    last = (i < nt - 1).astype(F32)
    pad_ref[pl.ds(0, SUBLANES), :] = _rms(xp_ref[0, 0], nrm) * first
    pad_ref[pl.ds(SUBLANES, tm), :] = h
    pad_ref[pl.ds(SUBLANES + tm, SUBLANES), :] = _rms(xn_ref[0, 0], nrm) * last
    h_prev = pad_ref[pl.ds(SUBLANES - 1, tm), :]
    h_next = pad_ref[pl.ds(SUBLANES + 1, tm), :]
    hh = 0.5 * (h_prev + h_next) - h
    mix = lambda j: h + hh * mu_ref[j:j + 1, :]
    xr, xw, xk, xv, xa, xg = (mix(j) for j in range(6))
    r_ref[0] = _dot(xr, wrkv_ref[0])
    k_ref[0] = _dot(xk, wrkv_ref[1])
    v_ref[0] = _dot(xv, wrkv_ref[2])
    a_ref[0] = jax.nn.sigmoid(a0_ref[...] + _dot(_dot(xa, a1_ref[...]), a2_ref[...]))
    g_ref[0] = _dot(jax.nn.sigmoid(_dot(xg, g1_ref[...])), g2_ref[...])
    for direction, dst in ((0, lwf_ref), (1, lwb_ref)):
        z = w0_ref[direction:direction + 1, :] + _dot(jnp.tanh(_dot(xw, w1_ref[direction])), w2_ref[direction])
        dst[0] = -jnp.exp(_log_sigmoid(z) - 0.5)


def rwkv_in(x, norm, mu, w_rkv, w0, w1, w2, a0, a1, a2, g1, g2, *, tm):
    b, t, d = x.shape
    x8 = x.reshape(b, t // SUBLANES, SUBLANES, d)
    tb = tm // SUBLANES
    nb = t // SUBLANES
    rank_g = g1.shape[1]
    pad_g = -rank_g % LANES
    g1p = jnp.pad(g1, ((0, 0), (0, pad_g)))
    g2p = jnp.pad(g2, ((0, pad_g), (0, 0)))
    tile = pl.BlockSpec((1, tm, d), lambda i, j: (i, j, 0))
    out = jax.ShapeDtypeStruct((b, t, d), F32)
    weights = [norm.reshape(1, d), mu, _mx(w_rkv), _mx(w1), _mx(w2), w0,
               _mx(a1), _mx(a2), a0.reshape(1, d), _mx(g1p), _mx(g2p)]
    return pl.pallas_call(
        functools.partial(_rwkv_in_kernel, tm=tm),
        grid=(b, t // tm),
        in_specs=[tile,
                  pl.BlockSpec((1, 1, SUBLANES, d), lambda i, j: (i, jnp.maximum(j * tb - 1, 0), 0, 0)),
                  pl.BlockSpec((1, 1, SUBLANES, d), lambda i, j: (i, jnp.minimum((j + 1) * tb, nb - 1), 0, 0))]
                 + [_resident(w.shape) for w in weights],
        out_specs=[tile] * 7,
        out_shape=[out] * 7,
        scratch_shapes=[pltpu.VMEM((tm + 2 * SUBLANES, d), F32)],
        compiler_params=_params("parallel", "parallel"),
        name="rwkv_in",
    )(x, x8, x8, *weights)


def _rwkv_scan_kernel(r_ref, k_ref, v_ref, a_ref, lwf_ref, lwb_ref, kk_ref, ka_ref, rk_ref, lnw_ref, lnb_ref,
                      out_ref, y_ref, rq_ref, m_ref, sadd_ref, gam_ref, s_ref, *, seq, chunk):
    L = chunk
    nc = seq // L
    n = R_HEAD_DIM
    lane = lax.broadcasted_iota(jnp.int32, (L, 2 * n), 1)
    head0 = (lane < n).astype(F32)
    head1 = 1.0 - head0

    def head_sum(x):
        s0 = jnp.sum(x * head0, axis=1, keepdims=True)
        s1 = jnp.sum(x * head1, axis=1, keepdims=True)
        return s0 * head0 + s1 * head1

    def stack(x):
        return jnp.concatenate([x * head0, x * head1], axis=0)

    def load(c):
        r0 = pl.multiple_of(c * L, L)
        rows = pl.ds(r0, L)
        r, k, v, a = r_ref[0, rows, :], k_ref[0, rows, :], v_ref[0, rows, :], a_ref[0, rows, :]
        kk = k * kk_ref[...]
        kk = kk / jnp.maximum(jnp.sqrt(head_sum(kk * kk)), 1e-12)
        kp = k * (1.0 + (a - 1.0) * ka_ref[...])
        return rows, r, kp, v, -kk, kk * a

    def unstack(x):
        return x[:L] + x[L:]

    masks_l = [_scan_masks(L, reverse)[0] for reverse in (False, True)]
    masks = [_scan_masks(2 * L, reverse) for reverse in (False, True)]
    lw_refs = (lwf_ref, lwb_ref)
    cb = RWKV_BUILD_CHUNKS

    def build(i, carry):
        chunk_ids = [i * cb + j for j in range(cb)]
        loaded = [load(c) for c in chunk_ids]
        chains = [(d, j) for j in range(cb) for d in range(2)]
        per_chain = lambda f: [f(d, j, *loaded[j][1:]) for d, j in chains]
        each = lambda f, *lists: [f(*args) for args in zip(*lists)]

        lw = per_chain(lambda d, j, *_: lw_refs[d][0, loaded[j][0], :])
        cin = each(lambda dj, x: _scan_cumsum(masks_l[dj[0]], x), chains, lw)
        ctot = each(lambda dj, c_: c_[0:1, :] if dj[0] == 1 else c_[L - 1:L, :], chains, cin)
        e_out = each(lambda c_: jnp.exp(-c_), cin)
        e_end = each(lambda c_, t_: jnp.exp(t_ - c_), cin, ctot)
        a_hat = each(lambda dj, c_, l_: stack(loaded[dj[1]][4] * jnp.exp(c_ - l_)), chains, cin, lw)
        r_hat = each(lambda dj, c_: stack(loaded[dj[1]][1] * jnp.exp(c_)), chains, cin)
        x = each(lambda a_, r_: _mx(jnp.concatenate([a_, r_], axis=0)), a_hat, r_hat)
        yk = each(lambda dj, e_: _mx(jnp.concatenate([stack(loaded[dj[1]][5] * e_), stack(loaded[dj[1]][2] * e_)],
                                                     axis=0)), chains, e_out)
        gram = each(_dot_nt, x, yk)
        a_ab = each(lambda dj, g_: jnp.where(masks[dj[0]][1], g_[:2 * L, :2 * L], 0.0), chains, gram)
        a_ak = each(lambda dj, g_: _mx(jnp.where(masks[dj[0]][1], g_[:2 * L, 2 * L:], 0.0)), chains, gram)
        a_rb = each(lambda dj, g_: _mx(jnp.where(masks[dj[0]][0], g_[2 * L:, :2 * L], 0.0)), chains, gram)
        a_rk = each(lambda dj, g_: _mx(jnp.where(masks[dj[0]][0], g_[2 * L:, 2 * L:], 0.0)), chains, gram)

        eye = (lax.broadcasted_iota(jnp.int32, (2 * L, 2 * L), 0)
               == lax.broadcasted_iota(jnp.int32, (2 * L, 2 * L), 1)).astype(F32)
        t_inv = each(lambda n_: eye + n_, a_ab)
        power = each(_mx, a_ab)
        span = 2
        while span < L:
            power_f32 = each(_dot, power, power)
            power = each(_mx, power_f32)
            t_inv = each(lambda t_, p_: t_ + _dot(t_, p_), t_inv, power)
            span *= 2

        v2 = [_mx(stack(ld[3])) for ld in loaded]
        av = each(lambda dj, a_: _dot(a_, v2[dj[1]]), chains, a_ak)
        w = each(lambda t_, a_, av_: _dot(t_, jnp.concatenate([_mx(a_), _mx(av_)], axis=1)), t_inv, a_hat, av)
        w_mx = each(_mx, w)
        z = each(_dot, a_rb, w_mx)
        yv = each(lambda dj, a_: _dot(a_, v2[dj[1]]), chains, a_rk)
        b_end = each(lambda dj, e_: _mx(stack(loaded[dj[1]][5] * e_)), chains, e_end)
        k_end = each(lambda dj, e_: _mx(stack(loaded[dj[1]][2] * e_)), chains, e_end)
        m_lr = each(lambda w_, b_: _dot_tn(w_[:, :2 * n], b_), w_mx, b_end)
        s_add = each(lambda dj, w_, b_, k_: _dot_tn(jnp.concatenate([w_[:, 2 * n:], v2[dj[1]]], axis=0),
                                                    jnp.concatenate([b_, k_], axis=0)), chains, w_mx, b_end, k_end)
        for idx, (d, j) in enumerate(chains):
            rows = loaded[j][0]
            y_ref[d, rows, :] = unstack(z[idx][:, 2 * n:] + yv[idx])
            rq_ref[d, rows, :] = unstack(r_hat[idx] + z[idx][:, :2 * n]).astype(rq_ref.dtype)
            m_ref[d, chunk_ids[j]] = m_lr[idx].astype(m_ref.dtype)
            sadd_ref[d, chunk_ids[j]] = s_add[idx]
            gam_ref[d, pl.ds(chunk_ids[j], 1), :] = jnp.exp(ctot[idx])
        return carry

    lax.fori_loop(0, nc // cb, build, 0)

    s_ref[...] = jnp.zeros_like(s_ref)

    def carry_state(direction, c):
        rows = pl.ds(pl.multiple_of(c * L, L), L)
        s_t = s_ref[direction]
        s_mx = _mx(s_t)
        y_ref[direction, rows, :] = y_ref[direction, rows, :] + _dot_nt(rq_ref[direction, rows, :], s_mx)
        s_ref[direction] = (s_t * gam_ref[direction, pl.ds(c, 1), :] + _dot(s_mx, m_ref[direction, c])
                            + sadd_ref[direction, c])

    def sweep(c, carry):
        carry_state(0, c)
        carry_state(1, nc - 1 - c)
        return carry

    lax.fori_loop(0, nc, sweep, 0)

    def epilogue(c, carry):
        rows, r, kp, v, _, _ = load(c)
        y = y_ref[0, rows, :] + y_ref[1, rows, :]
        mean = head_sum(y) * (1.0 / n)
        yc = y - mean
        var = head_sum(yc * yc) * (1.0 / n)
        yn = yc * lax.rsqrt(var + R_GN_EPS) * lnw_ref[...] + lnb_ref[...]
        out_ref[0, rows, :] = yn + head_sum(r * kp * rk_ref[...]) * v
        return carry

    lax.fori_loop(0, nc, epilogue, 0)


def rwkv_scan(r, k, v, a, lwf, lwb, k_k, k_a, r_k, ln_w, ln_b, *, chunk):
    b, t, d = r.shape
    w = 2 * R_HEAD_DIM
    pairs = d // w
    nc = t // chunk
    blk = pl.BlockSpec((1, t, w), lambda i, h: (i, 0, h))
    vec = pl.BlockSpec((1, w), lambda i, h: (0, h))
    row = lambda z: z.reshape(1, d)
    return pl.pallas_call(
        functools.partial(_rwkv_scan_kernel, seq=t, chunk=chunk),
        grid=(b, pairs),
        in_specs=[blk] * 6 + [vec] * 5,
        out_specs=blk,
        out_shape=jax.ShapeDtypeStruct((b, t, d), F32),
        scratch_shapes=[pltpu.VMEM((2, t, w), F32),
                        pltpu.VMEM((2, t, w), MXU_DTYPE),
                        pltpu.VMEM((2, nc, w, w), MXU_DTYPE),
                        pltpu.VMEM((2, nc, w, w), F32),
                        pltpu.VMEM((2, nc, w), F32),
                        pltpu.VMEM((2, w, w), F32)],
        compiler_params=_params("parallel", "parallel"),
        name="rwkv_scan",
    )(r, k, v, a, lwf, lwb, row(k_k), row(k_a), row(r_k), row(ln_w), row(ln_b))


def _xattn_kernel(x_ref, nrm_ref, wq_ref, k_ref, v_ref, wo_ref, o_ref):
    x = x_ref[0]
    d = x.shape[-1]
    dh = d // X_HEADS
    q = _mx(_dot(_rms(x, nrm_ref[...]), wq_ref[...]))
    heads = []
    for hd in range(X_HEADS):
        cols = slice(hd * dh, (hd + 1) * dh)
        s = _dot_nt(q[:, cols], k_ref[0, :, cols]) * (dh ** -0.5)
        p = jnp.exp(s - jnp.max(s, axis=-1, keepdims=True))
        p = p / jnp.sum(p, axis=-1, keepdims=True)
        heads.append(_mx(_dot(p, v_ref[0, :, cols])))
    o_ref[0] = x + _dot(jnp.concatenate(heads, axis=1), wo_ref[...])


def cross_attention(x, kv, norm, wq, wo, *, tm):
    b, t, d = x.shape
    m = kv.shape[1]
    tile = pl.BlockSpec((1, tm, d), lambda i, j: (i, j, 0))
    return pl.pallas_call(
        _xattn_kernel,
        grid=(b, t // tm),
        in_specs=[tile, _resident((1, d)), _resident((d, d)),
                  pl.BlockSpec((1, m, d), lambda i, j: (i, 0, 0)),
                  pl.BlockSpec((1, m, d), lambda i, j: (i, 0, 1)),
                  _resident((d, d))],
        out_specs=tile,
        out_shape=jax.ShapeDtypeStruct((b, t, d), F32),
        compiler_params=_params("parallel", "parallel"),
        name="cross_attention",
    )(x, norm.reshape(1, d), _mx(wq), kv, kv, _mx(wo))


def _ffn_kernel(x_ref, nrm_ref, wg_ref, wu_ref, wd_ref, fin_ref, o_ref, *, final_norm):
    x = x_ref[...]
    h = _mx(_rms(x, nrm_ref[...]))
    act = _silu(_dot(h, wg_ref[...])) * _dot(h, wu_ref[...])
    y = x + _dot(act, wd_ref[...])
    o_ref[...] = _rms(y, fin_ref[...]) if final_norm else y


def swiglu_ffn(x2d, norm, w_gate, w_up, w_down, final_gain, *, tm, final_norm):
    n, d = x2d.shape
    dff = w_gate.shape[1]
    row = pl.BlockSpec((tm, d), lambda i: (i, 0))
    return pl.pallas_call(
        functools.partial(_ffn_kernel, final_norm=final_norm),
        grid=(n // tm,),
        in_specs=[row, _resident((1, d)), _resident((d, dff)), _resident((d, dff)), _resident((dff, d)),
                  _resident((1, d))],
        out_specs=row,
        out_shape=jax.ShapeDtypeStruct((n, d), F32),
        compiler_params=_params("parallel"),
        name="swiglu_ffn",
    )(x2d, norm.reshape(1, d), _mx(w_gate), _mx(w_up), _mx(w_down), final_gain.reshape(1, d))


def _pack_even_in(w_in):
    d = w_in.shape[0]
    mw, gkw, gvw = M_HEADS * M_HEAD_DIM, G_HEADS * G_KEY_DIM, G_HEADS * G_VAL_DIM
    widths = (mw, mw, mw, mw, 2 * M_HEADS, 2 * M_HEADS, gkw, gkw, gvw, gvw, 2 * G_DECAY_RANK)
    starts = [0]
    for wdt in widths:
        starts.append(starts[-1] + wdt)
    mq, mk, mv, mo, mi, mf, gq, gk, gv, gg, glr = (w_in[:, starts[j]:starts[j + 1]] for j in range(len(widths)))
    gate = jnp.concatenate([mi, mf, glr], axis=1)
    gate = jnp.pad(gate, ((0, 0), (0, LANES - gate.shape[1])))
    packed = jnp.concatenate([mq, mk, mv, mo, gq, gk, gv, gg, gate], axis=1)
    assert packed.shape == (d, _EV_BLOCKS * LANES)
    return _mx(packed)


def _even_mixer_layer(x, norm, w_in, conv_qk, ig_bias, fg_bias, m_norm, decay_w2, decay_b, g_norm, w_out, *, tm):
    b, t, d = x.shape
    x2d = x.reshape(b * t, d)
    p = rms_matmul(x2d, norm, _pack_even_in(w_in), tm=tm, out_dtype=F32).reshape(b, t, -1)
    gate_bias = jnp.concatenate([ig_bias.reshape(-1), fg_bias.reshape(-1)])
    gate_bias = jnp.pad(gate_bias, (0, LANES - gate_bias.shape[0])).reshape(1, LANES)
    hm = mlstm_mixer(p, conv_qk, gate_bias, m_norm, chunk=min(MLSTM_CHUNK, t))
    og = gla_mixer(p, decay_w2, decay_b, g_norm, chunk=min(GLA_CHUNK, t))
    x2d = even_out_proj(x2d, hm.reshape(b * t, -1), og.reshape(b * t, -1), w_out, tm=tm)
    return x2d.reshape(b, t, d)


def _rwkv_mixer_layer(x, norm, mu, w_rkv, w0, w1, w2, a0, a1, a2, g1, g2, k_k, k_a, r_k, ln_w, ln_b, w_o, *, tm):
    b, t, d = x.shape
    r, k, v, a, lwf, lwb, g = rwkv_in(x, norm, mu, w_rkv, w0, w1, w2, a0, a1, a2, g1, g2, tm=min(tm, 256))
    y = rwkv_scan(r, k, v, a, lwf, lwb, k_k, k_a, r_k.reshape(-1), ln_w, ln_b, chunk=min(RWKV_CHUNK, t))
    x2d = gated_out_proj(x.reshape(b * t, d), y.reshape(b * t, d), g.reshape(b * t, d), w_o, tm=tm)
    return x2d.reshape(b, t, d)


def kernel(x, mem, norm_mix, norm_xattn, norm_mem, norm_ffn, norm_final, xa_wq, xa_wkv, xa_wo, ffn_w_gate, ffn_w_up, ffn_w_down, ev_w_in, ev_conv_qk, ev_m_ig_bias, ev_m_fg_bias, ev_m_norm, ev_g_decay_w2, ev_g_decay_b, ev_g_norm, ev_w_out, od_mu, od_w_rkv, od_w0, od_w1, od_w2, od_a0, od_a1, od_a2, od_g1, od_g2, od_k_k, od_k_a, od_r_k, od_ln_w, od_ln_b, od_w_o):
    b, t, d = x.shape
    m = mem.shape[1]
    depth = norm_mix.shape[0]
    tm = min(512, t)
    for layer in range(depth):
        if layer % 2 == 0:
            e = layer // 2
            x = _even_mixer_layer(x, norm_mix[layer], ev_w_in[e], ev_conv_qk[e], ev_m_ig_bias[e], ev_m_fg_bias[e],
                                  ev_m_norm[e], ev_g_decay_w2[e], ev_g_decay_b[e], ev_g_norm[e], ev_w_out[e], tm=tm)
        else:
            o = layer // 2
            x = _rwkv_mixer_layer(x, norm_mix[layer], od_mu[o], od_w_rkv[o], od_w0[o], od_w1[o], od_w2[o], od_a0[o],
                                  od_a1[o], od_a2[o], od_g1[o], od_g2[o], od_k_k[o], od_k_a[o], od_r_k[o],
                                  od_ln_w[o], od_ln_b[o], od_w_o[o], tm=tm)
        kv = rms_matmul(mem.reshape(b * m, d), norm_mem[layer], _mx(xa_wkv[layer]), tm=min(512, b * m),
                        out_dtype=MXU_DTYPE).reshape(b, m, 2 * d)
        x = cross_attention(x, kv, norm_xattn[layer], xa_wq[layer], xa_wo[layer], tm=tm)
        last = layer == depth - 1
        x = swiglu_ffn(x.reshape(b * t, d), norm_ffn[layer], ffn_w_gate[layer], ffn_w_up[layer], ffn_w_down[layer],
                       norm_final, tm=tm, final_norm=last).reshape(b, t, d)
    return x
```

```python
import functools

import jax
import jax.numpy as jnp
from jax import lax
from jax.experimental import pallas as pl
from jax.experimental.pallas import tpu as pltpu

F32 = jnp.float32
MXU_DTYPE = jnp.bfloat16
LANES = 128
SUBLANES = 8
VMEM_LIMIT_BYTES = 60 * 1024 * 1024
NEG_BIG = -1e30

EPS = 1e-6
CONV_WIDTH = 3
M_HEADS = 4
M_HEAD_DIM = 128
G_HEADS = 4
G_KEY_DIM = 64
G_VAL_DIM = 128
G_DECAY_RANK = 16
G_DECAY_TAU = 16.0
R_HEAD_DIM = 64
R_GN_EPS = 64e-5
X_HEADS = 4

MLSTM_CHUNK = 256
GLA_CHUNK = 64
RWKV_CHUNK = 64
MLSTM_GROUP_CHUNKS = 2
GLA_GROUP_CHUNKS = 4
RWKV_BUILD_CHUNKS = 4

_EV_MQ, _EV_MK, _EV_MV, _EV_MO = 0, 4, 8, 12
_EV_GQ, _EV_GK, _EV_GV, _EV_GG, _EV_GATE = 16, 18, 20, 24, 28
_EV_BLOCKS = 29
_GATE_IG, _GATE_FG, _GATE_LR = 0, 8, 16


def _mx(a):
    return a.astype(MXU_DTYPE)


def _dot(a, b):
    return jnp.dot(_mx(a), _mx(b), preferred_element_type=F32)


def _dot_nt(a, b):
    return lax.dot_general(_mx(a), _mx(b), (((1,), (1,)), ((), ())), preferred_element_type=F32)


def _dot_tn(a, b):
    return lax.dot_general(_mx(a), _mx(b), (((0,), (0,)), ((), ())), preferred_element_type=F32)


def _scan_cumsum(mask, x):
    m = _mx(mask)
    hi = _mx(x)
    rest = x - hi.astype(F32)
    mid = _mx(rest)
    lo = _mx(rest - mid.astype(F32))
    dot = lambda t: jnp.dot(m, t, preferred_element_type=F32)
    return dot(hi) + dot(mid) + dot(lo)


def _rms(x, g):
    return x * lax.rsqrt(jnp.mean(x * x, axis=-1, keepdims=True) + EPS) * g


def _log_sigmoid(x):
    return jnp.minimum(x, 0.0) - jnp.log1p(jnp.exp(-jnp.abs(x)))


def _silu(x):
    return x * jax.nn.sigmoid(x)


def _params(*sem):
    return pltpu.CompilerParams(dimension_semantics=sem, vmem_limit_bytes=VMEM_LIMIT_BYTES)


def _resident(shape):
    nd = len(shape)
    return pl.BlockSpec(shape, lambda *_: (0,) * nd, pipeline_mode=pl.Buffered(1))


def _scan_masks(n, reverse):
    row = lax.broadcasted_iota(jnp.int32, (n, n), 0)
    col = lax.broadcasted_iota(jnp.int32, (n, n), 1)
    if reverse:
        return col >= row, col > row
    return col <= row, col < row


def _rms_matmul_kernel(x_ref, g_ref, w_ref, o_ref):
    h = _rms(x_ref[...], g_ref[...])
    o_ref[...] = _dot(h, w_ref[...]).astype(o_ref.dtype)


def rms_matmul(x2d, g, w, *, tm, out_dtype):
    n, d = x2d.shape
    nout = w.shape[1]
    return pl.pallas_call(
        _rms_matmul_kernel,
        grid=(n // tm,),
        in_specs=[pl.BlockSpec((tm, d), lambda i: (i, 0)), _resident((1, d)), _resident((d, nout))],
        out_specs=pl.BlockSpec((tm, nout), lambda i: (i, 0)),
        out_shape=jax.ShapeDtypeStruct((n, nout), out_dtype),
        compiler_params=_params("parallel"),
        name="rms_matmul",
    )(x2d, g.reshape(1, d), w)


def _extract_col(x, lane_idx):
    lane = lax.broadcasted_iota(jnp.int32, x.shape, 1)
    return jnp.sum(jnp.where(lane == lane_idx, x, 0.0), axis=1, keepdims=True)


def _mlstm_kernel(q_ref, k_ref, v_ref, o_ref, gate_ref, conv_ref, gbias_ref, norm_ref, out_ref,
                  qc_ref, kc_ref, gl_ref, hf_ref, hb_ref, c_ref, m_ref, *, seq, chunk):
    L = chunk
    nc = seq // L
    d = M_HEAD_DIM
    head = pl.program_id(1)

    row_id = lax.broadcasted_iota(jnp.int32, (L, d), 0)
    lane_id = lax.broadcasted_iota(jnp.int32, (L, LANES), 1)
    is_fg = (lane_id >= _GATE_FG) & (lane_id < _GATE_LR)

    def conv_silu(src_ref, w, r0, c):
        x = src_ref[0, pl.ds(r0, L), :]
        prev_row = src_ref[0, pl.ds(jnp.maximum(r0 - 1, 0), 1), :] * jnp.where(c > 0, 1.0, 0.0)
        next_row = src_ref[0, pl.ds(jnp.minimum(r0 + L, seq - 1), 1), :] * jnp.where(c < nc - 1, 1.0, 0.0)
        x_prev = jnp.where(row_id == 0, prev_row, pltpu.roll(x, 1, axis=0))
        x_next = jnp.where(row_id == L - 1, next_row, pltpu.roll(x, L - 1, axis=0))
        return _silu(x_prev * w[0:1, :] + x * w[1:2, :] + x_next * w[2:3, :])

    def prologue(c, carry):
        r0 = pl.multiple_of(c * L, L)
        qc_ref[pl.ds(r0, L), :] = (conv_silu(q_ref, conv_ref[0], r0, c) * (d ** -0.5)).astype(qc_ref.dtype)
        kc_ref[pl.ds(r0, L), :] = conv_silu(k_ref, conv_ref[1], r0, c)
        g = gate_ref[0, pl.ds(r0, L), :] + gbias_ref[...]
        gl_ref[pl.ds(r0, L), :] = jnp.where(is_fg, _log_sigmoid(g), g)
        return carry

    lax.fori_loop(0, nc, prologue, 0)

    c_ref[...] = jnp.zeros_like(c_ref)
    m_ref[...] = jnp.zeros_like(m_ref)
    ones = jnp.ones((L, d), MXU_DTYPE)

    masks = [_scan_masks(L, reverse)[0] for reverse in (False, True)]
    dst_refs = (hf_ref, hb_ref)
    cb = MLSTM_GROUP_CHUNKS
    lane_sel = lax.broadcasted_iota(jnp.int32, (L, LANES), 1)

    def split_lanes(col, first):
        hi = _mx(col).astype(F32)
        mid = _mx(col - hi).astype(F32)
        lo = col - hi - mid
        out = jnp.where(lane_sel < 6, 1.0, 0.0)
        for offset, term in enumerate((hi, mid, lo)):
            out = jnp.where(lane_sel == first + offset, term, out)
        return _mx(out)

    def body(i, carry):
        chains = [(dr, j) for j in range(cb) for dr in range(2)]
        chunk_of = lambda dr, j: (i * cb + j) if dr == 0 else (nc - 1 - (i * cb + j))
        rows = [pl.ds(pl.multiple_of(chunk_of(dr, j) * L, L), L) for dr, j in chains]
        each = lambda f, *lists: [f(*args) for args in zip(*lists)]

        g = each(lambda r_: gl_ref[r_, :], rows)
        bcum = each(lambda dj, g_: _scan_cumsum(masks[dj[0]], g_), chains, g)
        ig_col = each(lambda dj, g_: _extract_col(g_, _GATE_IG + dj[0] * M_HEADS + head), chains, g)
        b_col = each(lambda dj, b_: _extract_col(b_, _GATE_FG + dj[0] * M_HEADS + head), chains, bcum)
        g_tot = each(lambda dj, b_: b_[0:1, :] if dj[0] == 1 else b_[L - 1:L, :], chains, b_col)
        dmat = each(lambda dj, b_, i_: jnp.where(masks[dj[0]], _dot_nt(split_lanes(b_, 0), split_lanes(i_ - b_, 3)),
                                                 NEG_BIG), chains, b_col, ig_col)
        dmax = each(lambda d_: jnp.max(d_, axis=1, keepdims=True), dmat)
        q = each(lambda r_: qc_ref[r_, :], rows)
        k = each(lambda r_: kc_ref[r_, :], rows)
        v_aug = each(lambda r_: jnp.concatenate([_mx(v_ref[0, r_, :]), ones], axis=1), rows)
        sc = each(lambda q_, k_, d_, m_: _mx(_dot_nt(q_, k_) * jnp.exp(d_ - m_)), q, k, dmat, dmax)
        nd_intra = each(_dot, sc, v_aug)
        kw_log = each(lambda g_, b_, i_: g_ - b_ + i_, g_tot, b_col, ig_col)
        kw_max = each(lambda w_: jnp.max(w_, axis=0, keepdims=True), kw_log)
        d_c = each(lambda k_, w_, a_, v_: _dot_tn(_mx(k_ * jnp.exp(w_ - a_)), v_), k, kw_log, kw_max, v_aug)

        c_start = [None] * len(chains)
        m_start = [None] * len(chains)
        for dr in range(2):
            c_cur = c_ref[dr]
            m_cur = m_ref[dr][:, 0:1]
            for j in range(cb):
                idx = chains.index((dr, j))
                c_start[idx], m_start[idx] = _mx(c_cur), m_cur
                m_new = jnp.maximum(g_tot[idx] + m_cur, kw_max[idx])
                c_cur = jnp.exp(g_tot[idx] + m_cur - m_new) * c_cur + jnp.exp(kw_max[idx] - m_new) * d_c[idx]
                m_cur = m_new
            c_ref[dr] = c_cur
            m_ref[dr] = jnp.broadcast_to(m_cur, (1, LANES))

        nd_inter = each(_dot, q, c_start)
        for idx, (dr, j) in enumerate(chains):
            inter_log = b_col[idx] + m_start[idx]
            m_t = jnp.maximum(inter_log, dmax[idx])
            numden = jnp.exp(dmax[idx] - m_t) * nd_intra[idx] + jnp.exp(inter_log - m_t) * nd_inter[idx]
            dst_refs[dr][rows[idx], :] = numden[:, :d] / jnp.maximum(jnp.abs(numden[:, d:]), jnp.exp(-m_t))
        return carry

    lax.fori_loop(0, nc // cb, body, 0)

    def epilogue(c, carry):
        r0 = pl.multiple_of(c * L, L)
        h = hf_ref[pl.ds(r0, L), :] + hb_ref[pl.ds(r0, L), :]
        h = _rms(h, norm_ref[...])
        out_ref[0, pl.ds(r0, L), :] = h * jax.nn.sigmoid(o_ref[0, pl.ds(r0, L), :])
        return carry

    lax.fori_loop(0, nc, epilogue, 0)


def mlstm_mixer(p, conv_qk, gate_bias, m_norm, *, chunk):
    b, t, _ = p.shape
    d = M_HEAD_DIM
    blk = lambda off: pl.BlockSpec((1, t, d), lambda i, h, off=off: (i, 0, off + h))
    conv = conv_qk.reshape(CONV_WIDTH, 2, M_HEADS, d).transpose(2, 1, 0, 3)
    return pl.pallas_call(
        functools.partial(_mlstm_kernel, seq=t, chunk=chunk),
        grid=(b, M_HEADS),
        in_specs=[blk(_EV_MQ), blk(_EV_MK), blk(_EV_MV), blk(_EV_MO),
                  pl.BlockSpec((1, t, LANES), lambda i, h: (i, 0, _EV_GATE)),
                  pl.BlockSpec((None, 2, CONV_WIDTH, d), lambda i, h: (h, 0, 0, 0)),
                  pl.BlockSpec((1, LANES), lambda i, h: (0, 0)),
                  pl.BlockSpec((1, d), lambda i, h: (0, h))],
        out_specs=pl.BlockSpec((1, t, d), lambda i, h: (i, 0, h)),
        out_shape=jax.ShapeDtypeStruct((b, t, M_HEADS * d), F32),
        scratch_shapes=[pltpu.VMEM((t, d), MXU_DTYPE),
                        pltpu.VMEM((t, d), F32),
                        pltpu.VMEM((t, LANES), F32),
                        pltpu.VMEM((t, d), F32),
                        pltpu.VMEM((t, d), F32),
                        pltpu.VMEM((2, d, 2 * d), F32),
                        pltpu.VMEM((2, 1, LANES), F32)],
        compiler_params=_params("parallel", "parallel"),
        name="mlstm",
    )(p, p, p, p, p, conv, gate_bias, m_norm.reshape(1, M_HEADS * d))


def _gla_kernel(q_ref, k_ref, v0_ref, v1_ref, g0_ref, g1_ref, gate_ref, w2_ref, b2_ref, norm_ref, out_ref,
                of_ref, ob_ref, s_ref, *, seq, chunk):
    L = chunk
    nc = seq // L
    dk2 = 2 * G_KEY_DIM
    dv = G_VAL_DIM
    lane_k = lax.broadcasted_iota(jnp.int32, (L, dk2), 1)
    head0 = (lane_k < G_KEY_DIM).astype(F32)
    head1 = 1.0 - head0
    s_row = lax.broadcasted_iota(jnp.int32, (2 * dv, dk2), 0)
    s_col = lax.broadcasted_iota(jnp.int32, (2 * dv, dk2), 1)
    s_mask = ((s_row < dv) == (s_col < G_KEY_DIM)).astype(F32)

    s_ref[...] = jnp.zeros_like(s_ref)
    masks = [_scan_masks(L, reverse)[0] for reverse in (False, True)]
    masks2 = [jnp.concatenate([m, m], axis=0) for m in masks]
    dst_refs = (of_ref, ob_ref)
    cb = GLA_GROUP_CHUNKS

    def body(i, carry):
        chains = [(d, j) for j in range(cb) for d in range(2)]
        chunk_of = lambda d, j: (i * cb + j) if d == 0 else (nc - 1 - (i * cb + j))
        rows = [pl.ds(pl.multiple_of(chunk_of(d, j) * L, L), L) for d, j in chains]
        each = lambda f, *lists: [f(*args) for args in zip(*lists)]

        z = each(lambda dj, r_: _dot(gate_ref[0, r_, :], w2_ref[dj[0], 0]) + b2_ref[dj[0], 0], chains, rows)
        la = each(lambda z_: _log_sigmoid(z_) * (1.0 / G_DECAY_TAU), z)
        bcum = each(lambda dj, l_: _scan_cumsum(masks[dj[0]], l_), chains, la)
        g_tot = each(lambda dj, b_: b_[0:1, :] if dj[0] == 1 else b_[L - 1:L, :], chains, bcum)
        mid = each(lambda b_: b_[L // 2:L // 2 + 1, :], bcum)
        q = each(lambda r_: q_ref[0, r_, :] * (G_KEY_DIM ** -0.5), rows)
        k = each(lambda r_: k_ref[0, r_, :], rows)
        v = each(lambda r_: jnp.concatenate([_mx(v0_ref[0, r_, :]), _mx(v1_ref[0, r_, :])], axis=1), rows)
        q_in = each(lambda q_, b_, m_: q_ * jnp.exp(b_ - m_), q, bcum, mid)
        q_in2 = each(lambda q_: _mx(jnp.concatenate([q_ * head0, q_ * head1], axis=0)), q_in)
        k_in = each(lambda k_, b_, m_: _mx(k_ * jnp.exp(m_ - b_)), k, bcum, mid)
        scores = each(lambda dj, q_, k_: _mx(jnp.where(masks2[dj[0]], _dot_nt(q_, k_), 0.0)), chains, q_in2, k_in)
        o_intra = each(lambda a_, v_: jnp.concatenate([_dot(a_[:L], v_[:, :dv]), _dot(a_[L:], v_[:, dv:])], axis=1),
                       scores, v)
        q_hat = each(lambda q_, b_: _mx(q_ * jnp.exp(b_)), q, bcum)
        k_hat = each(lambda k_, b_, g_: _mx(k_ * jnp.exp(g_ - b_)), k, bcum, g_tot)
        d_s = each(lambda v_, k_: _dot_tn(v_, k_) * s_mask, v, k_hat)
        decay = each(jnp.exp, g_tot)

        s_at_start = [None] * len(chains)
        for d in range(2):
            s_cur = s_ref[d]
            for j in range(cb):
                idx = chains.index((d, j))
                s_at_start[idx] = _mx(s_cur)
                s_cur = s_cur * decay[idx] + d_s[idx]
            s_ref[d] = s_cur
        o_inter = each(_dot_nt, q_hat, s_at_start)
        for idx, (d, j) in enumerate(chains):
            dst_refs[d][rows[idx], :] = o_intra[idx] + o_inter[idx]
        return carry

    lax.fori_loop(0, nc // cb, body, 0)

    def epilogue(c, carry):
        r0 = pl.multiple_of(c * L, L)
        o = of_ref[pl.ds(r0, L), :] + ob_ref[pl.ds(r0, L), :]
        g = jnp.concatenate([g0_ref[0, pl.ds(r0, L), :], g1_ref[0, pl.ds(r0, L), :]], axis=1)
        nrm = norm_ref[...]
        o = jnp.concatenate([_rms(o[:, :dv], nrm[:, :dv]), _rms(o[:, dv:], nrm[:, dv:])], axis=1)
        out_ref[0, pl.ds(r0, L), :] = o * _silu(g)
        return carry

    lax.fori_loop(0, nc, epilogue, 0)


def gla_mixer(p, decay_w2, decay_b, g_norm, *, chunk):
    b, t, _ = p.shape
    dv = G_VAL_DIM
    pairs = G_HEADS // 2
    dk2 = 2 * G_KEY_DIM
    w2 = jnp.zeros((2, LANES, pairs * dk2), F32)
    for direction in range(2):
        lo = _GATE_LR + direction * G_DECAY_RANK
        w2 = w2.at[direction, lo:lo + G_DECAY_RANK].set(decay_w2[direction])
    w2 = _mx(w2.reshape(2, LANES, pairs, dk2).transpose(0, 2, 1, 3))
    b2 = decay_b.reshape(2, pairs, 1, dk2)
    blk = lambda off, mult: pl.BlockSpec((1, t, LANES), lambda i, h, off=off, mult=mult: (i, 0, off + mult * h))
    return pl.pallas_call(
        functools.partial(_gla_kernel, seq=t, chunk=chunk),
        grid=(b, pairs),
        in_specs=[blk(_EV_GQ, 1), blk(_EV_GK, 1), blk(_EV_GV, 2), blk(_EV_GV + 1, 2),
                  blk(_EV_GG, 2), blk(_EV_GG + 1, 2),
                  pl.BlockSpec((1, t, LANES), lambda i, h: (i, 0, _EV_GATE)),
                  pl.BlockSpec((2, 1, LANES, dk2), lambda i, h: (0, h, 0, 0)),
                  pl.BlockSpec((2, 1, 1, dk2), lambda i, h: (0, h, 0, 0)),
                  pl.BlockSpec((1, 2 * dv), lambda i, h: (0, h))],
        out_specs=pl.BlockSpec((1, t, 2 * dv), lambda i, h: (i, 0, h)),
        out_shape=jax.ShapeDtypeStruct((b, t, G_HEADS * dv), F32),
        scratch_shapes=[pltpu.VMEM((t, 2 * dv), F32),
                        pltpu.VMEM((t, 2 * dv), F32),
                        pltpu.VMEM((2, 2 * dv, dk2), F32)],
        compiler_params=_params("parallel", "parallel"),
        name="gla",
    )(p, p, p, p, p, p, p, w2, b2, g_norm.reshape(1, G_HEADS * dv))


def _even_out_kernel(x_ref, hm_ref, og_ref, wm_ref, wg_ref, o_ref):
    o_ref[...] = x_ref[...] + _dot(hm_ref[...], wm_ref[...]) + _dot(og_ref[...], wg_ref[...])


def even_out_proj(x2d, hm, og, w_out, *, tm):
    n, d = x2d.shape
    wm, wg = _mx(w_out[:hm.shape[1]]), _mx(w_out[hm.shape[1]:])
    row = lambda width: pl.BlockSpec((tm, width), lambda i: (i, 0))
    return pl.pallas_call(
        _even_out_kernel,
        grid=(n // tm,),
        in_specs=[row(d), row(hm.shape[1]), row(og.shape[1]), _resident(wm.shape), _resident(wg.shape)],
        out_specs=row(d),
        out_shape=jax.ShapeDtypeStruct((n, d), F32),
        compiler_params=_params("parallel"),
        name="even_out_proj",
    )(x2d, hm, og, wm, wg)


def _gated_out_kernel(x_ref, y_ref, g_ref, w_ref, o_ref):
    o_ref[...] = x_ref[...] + _dot(y_ref[...] * g_ref[...], w_ref[...])


def gated_out_proj(x2d, y, g, w, *, tm):
    n, d = x2d.shape
    row = pl.BlockSpec((tm, d), lambda i: (i, 0))
    return pl.pallas_call(
        _gated_out_kernel,
        grid=(n // tm,),
        in_specs=[row, row, row, _resident(w.shape)],
        out_specs=row,
        out_shape=jax.ShapeDtypeStruct((n, d), F32),
        compiler_params=_params("parallel"),
        name="gated_out_proj",
    )(x2d, y, g, _mx(w))


def _rwkv_in_kernel(x_ref, xp_ref, xn_ref, nrm_ref, mu_ref, wrkv_ref, w1_ref, w2_ref, w0_ref,
                    a1_ref, a2_ref, a0_ref, g1_ref, g2_ref,
                    r_ref, k_ref, v_ref, a_ref, lwf_ref, lwb_ref, g_ref, pad_ref, *, tm):
    i = pl.program_id(1)
    nt = pl.num_programs(1)
    nrm = nrm_ref[...]
    h = _rms(x_ref[0], nrm)
    first = jnp.where(i > 0, 1.0, 0.0)
    last = jnp.where(i < nt - 1, 1.0, 0.0)
    pad_ref[pl.ds(0, SUBLANES), :] = _rms(xp_ref[0, 0], nrm) * first
    pad_ref[pl.ds(SUBLANES, tm), :] = h
    pad_ref[pl.ds(SUBLANES + tm, SUBLANES), :] = _rms(xn_ref[0, 0], nrm) * last
    h_prev = pad_ref[pl.ds(SUBLANES - 1, tm), :]
    h_next = pad_ref[pl.ds(SUBLANES + 1, tm), :]
    hh = 0.5 * (h_prev + h_next) - h
    mix = lambda j: h + hh * mu_ref[j:j + 1, :]
    xr, xw, xk, xv, xa, xg = (mix(j) for j in range(6))
    r_ref[0] = _dot(xr, wrkv_ref[0])
    k_ref[0] = _dot(xk, wrkv_ref[1])
    v_ref[0] = _dot(xv, wrkv_ref[2])
    a_ref[0] = jax.nn.sigmoid(a0_ref[...] + _dot(_dot(xa, a1_ref[...]), a2_ref[...]))
    g_ref[0] = _dot(jax.nn.sigmoid(_dot(xg, g1_ref[...])), g2_ref[...])
    for direction, dst in ((0, lwf_ref), (1, lwb_ref)):
        z = w0_ref[direction:direction + 1, :] + _dot(jnp.tanh(_dot(xw, w1_ref[direction])), w2_ref[direction])
        dst[0] = -jnp.exp(_log_sigmoid(z) - 0.5)


def rwkv_in(x, norm, mu, w_rkv, w0, w1, w2, a0, a1, a2, g1, g2, *, tm):
    b, t, d = x.shape
    x8 = x.reshape(b, t // SUBLANES, SUBLANES, d)
    tb = tm // SUBLANES
    nb = t // SUBLANES
    rank_g = g1.shape[1]
    pad_g = -rank_g % LANES
    g1p = jnp.pad(g1, ((0, 0), (0, pad_g)))
    g2p = jnp.pad(g2, ((0, pad_g), (0, 0)))
    tile = pl.BlockSpec((1, tm, d), lambda i, j: (i, j, 0))
    out = jax.ShapeDtypeStruct((b, t, d), F32)
    weights = [norm.reshape(1, d), mu, _mx(w_rkv), _mx(w1), _mx(w2), w0,
               _mx(a1), _mx(a2), a0.reshape(1, d), _mx(g1p), _mx(g2p)]
    return pl.pallas_call(
        functools.partial(_rwkv_in_kernel, tm=tm),
        grid=(b, t // tm),
        in_specs=[tile,
                  pl.BlockSpec((1, 1, SUBLANES, d), lambda i, j: (i, jnp.maximum(j * tb - 1, 0), 0, 0)),
                  pl.BlockSpec((1, 1, SUBLANES, d), lambda i, j: (i, jnp.minimum((j + 1) * tb, nb - 1), 0, 0))]
                 + [_resident(w.shape) for w in weights],
        out_specs=[tile] * 7,
        out_shape=[out] * 7,
        scratch_shapes=[pltpu.VMEM((tm + 2 * SUBLANES, d), F32)],
        compiler_params=_params("parallel", "parallel"),
        name="rwkv_in",
    )(x, x8, x8, *weights)


def _rwkv_scan_kernel(r_ref, k_ref, v_ref, a_ref, lwf_ref, lwb_ref, kk_ref, ka_ref, rk_ref, lnw_ref, lnb_ref,
                      out_ref, y_ref, rq_ref, m_ref, sadd_ref, gam_ref, s_ref, *, seq, chunk):
    L = chunk
    nc = seq // L
    n = R_HEAD_DIM
    lane = lax.broadcasted_iota(jnp.int32, (L, 2 * n), 1)
    head0 = (lane < n).astype(F32)
    head1 = 1.0 - head0

    def head_sum(x):
        s0 = jnp.sum(x * head0, axis=1, keepdims=True)
        s1 = jnp.sum(x * head1, axis=1, keepdims=True)
        return s0 * head0 + s1 * head1

    def stack(x):
        return jnp.concatenate([x * head0, x * head1], axis=0)

    def load(c):
        r0 = pl.multiple_of(c * L, L)
        rows = pl.ds(r0, L)
        r, k, v, a = r_ref[0, rows, :], k_ref[0, rows, :], v_ref[0, rows, :], a_ref[0, rows, :]
        kk = k * kk_ref[...]
        kk = kk / jnp.maximum(jnp.sqrt(head_sum(kk * kk)), 1e-12)
        kp = k * (1.0 + (a - 1.0) * ka_ref[...])
        return rows, r, kp, v, -kk, kk * a

    def unstack(x):
        return x[:L] + x[L:]

    masks_l = [_scan_masks(L, reverse)[0] for reverse in (False, True)]
    masks = [_scan_masks(2 * L, reverse) for reverse in (False, True)]
    lw_refs = (lwf_ref, lwb_ref)
    cb = RWKV_BUILD_CHUNKS

    def build(i, carry):
        chunk_ids = [i * cb + j for j in range(cb)]
        loaded = [load(c) for c in chunk_ids]
        chains = [(d, j) for j in range(cb) for d in range(2)]
        per_chain = lambda f: [f(d, j, *loaded[j][1:]) for d, j in chains]
        each = lambda f, *lists: [f(*args) for args in zip(*lists)]

        lw = per_chain(lambda d, j, *_: lw_refs[d][0, loaded[j][0], :])
        cin = each(lambda dj, x: _scan_cumsum(masks_l[dj[0]], x), chains, lw)
        ctot = each(lambda dj, c_: c_[0:1, :] if dj[0] == 1 else c_[L - 1:L, :], chains, cin)
        e_out = each(lambda c_: jnp.exp(-c_), cin)
        e_end = each(lambda c_, t_: jnp.exp(t_ - c_), cin, ctot)
        a_hat = each(lambda dj, c_, l_: stack(loaded[dj[1]][4] * jnp.exp(c_ - l_)), chains, cin, lw)
        r_hat = each(lambda dj, c_: stack(loaded[dj[1]][1] * jnp.exp(c_)), chains, cin)
        x = each(lambda a_, r_: _mx(jnp.concatenate([a_, r_], axis=0)), a_hat, r_hat)
        yk = each(lambda dj, e_: _mx(jnp.concatenate([stack(loaded[dj[1]][5] * e_), stack(loaded[dj[1]][2] * e_)],
                                                     axis=0)), chains, e_out)
        gram = each(_dot_nt, x, yk)
        a_ab = each(lambda dj, g_: jnp.where(masks[dj[0]][1], g_[:2 * L, :2 * L], 0.0), chains, gram)
        a_ak = each(lambda dj, g_: _mx(jnp.where(masks[dj[0]][1], g_[:2 * L, 2 * L:], 0.0)), chains, gram)
        a_rb = each(lambda dj, g_: _mx(jnp.where(masks[dj[0]][0], g_[2 * L:, :2 * L], 0.0)), chains, gram)
        a_rk = each(lambda dj, g_: _mx(jnp.where(masks[dj[0]][0], g_[2 * L:, 2 * L:], 0.0)), chains, gram)

        eye = (lax.broadcasted_iota(jnp.int32, (2 * L, 2 * L), 0)
               == lax.broadcasted_iota(jnp.int32, (2 * L, 2 * L), 1)).astype(F32)
        t_inv = each(lambda n_: eye + n_, a_ab)
        power = each(lambda n_: _mx(_dot(n_, n_)), a_ab)
        for _ in range(L.bit_length() - 3):
            both = each(lambda p_, t_: _dot(p_, jnp.concatenate([p_, _mx(t_)], axis=1)), power, t_inv)
            power = each(lambda b_: _mx(b_[:, :2 * L]), both)
            t_inv = each(lambda t_, b_: t_ + b_[:, 2 * L:], t_inv, both)
        t_inv = each(lambda p_, t_: t_ + _dot(p_, t_), power, t_inv)

        v2 = [_mx(stack(ld[3])) for ld in loaded]
        av = each(lambda dj, a_: _dot(a_, v2[dj[1]]), chains, a_ak)
        w = each(lambda t_, a_, av_: _dot(t_, jnp.concatenate([_mx(a_), _mx(av_)], axis=1)), t_inv, a_hat, av)
        w_mx = each(_mx, w)
        no_v = jnp.zeros((2 * L, 2 * n), MXU_DTYPE)
        z = each(lambda dj, b_, k_, w_: _dot(jnp.concatenate([b_, k_], axis=1),
                                             jnp.concatenate([w_, jnp.concatenate([no_v, v2[dj[1]]], axis=1)], axis=0)),
                 chains, a_rb, a_rk, w_mx)
        b_end = each(lambda dj, e_: _mx(stack(loaded[dj[1]][5] * e_)), chains, e_end)
        k_end = each(lambda dj, e_: _mx(stack(loaded[dj[1]][2] * e_)), chains, e_end)
        m_lr = each(lambda w_, b_: _dot_tn(w_[:, :2 * n], b_), w_mx, b_end)
        s_add = each(lambda dj, w_, b_, k_: _dot_tn(jnp.concatenate([w_[:, 2 * n:], v2[dj[1]]], axis=0),
                                                    jnp.concatenate([b_, k_], axis=0)), chains, w_mx, b_end, k_end)
        for idx, (d, j) in enumerate(chains):
            rows = loaded[j][0]
            y_ref[d, rows, :] = unstack(z[idx][:, 2 * n:])
            rq_ref[d, rows, :] = unstack(r_hat[idx] + z[idx][:, :2 * n]).astype(rq_ref.dtype)
            m_ref[d, chunk_ids[j]] = m_lr[idx].astype(m_ref.dtype)
            sadd_ref[d, chunk_ids[j]] = s_add[idx]
            gam_ref[d, pl.ds(chunk_ids[j], 1), :] = jnp.exp(ctot[idx])
        return carry

    lax.fori_loop(0, nc // cb, build, 0)

    s_ref[...] = jnp.zeros_like(s_ref)

    def carry_state(direction, c):
        rows = pl.ds(pl.multiple_of(c * L, L), L)
        s_t = s_ref[direction]
        s_mx = _mx(s_t)
        y = y_ref[direction, rows, :] + _dot_nt(rq_ref[direction, rows, :], s_mx)
        s_ref[direction] = (s_t * gam_ref[direction, pl.ds(c, 1), :] + _dot(s_mx, m_ref[direction, c])
                            + sadd_ref[direction, c])
        return y

    def finish(c, y):
        rows = pl.ds(pl.multiple_of(c * L, L), L)
        r, k, v, a = r_ref[0, rows, :], k_ref[0, rows, :], v_ref[0, rows, :], a_ref[0, rows, :]
        kp = k * (1.0 + (a - 1.0) * ka_ref[...])
        mean = head_sum(y) * (1.0 / n)
        yc = y - mean
        var = head_sum(yc * yc) * (1.0 / n)
        yn = yc * lax.rsqrt(var + R_GN_EPS) * lnw_ref[...] + lnb_ref[...]
        out_ref[0, rows, :] = yn + head_sum(r * kp * rk_ref[...]) * v

    half = nc // 2

    def sweep_outward(c, carry):
        yf = carry_state(0, c)
        yb = carry_state(1, nc - 1 - c)
        y_ref[0, pl.ds(pl.multiple_of(c * L, L), L), :] = yf
        y_ref[1, pl.ds(pl.multiple_of((nc - 1 - c) * L, L), L), :] = yb
        return carry

    def sweep_crossed(c, carry):
        yf = carry_state(0, c)
        yb = carry_state(1, nc - 1 - c)
        finish(c, yf + y_ref[1, pl.ds(pl.multiple_of(c * L, L), L), :])
        finish(nc - 1 - c, yb + y_ref[0, pl.ds(pl.multiple_of((nc - 1 - c) * L, L), L), :])
        return carry

    lax.fori_loop(0, half, sweep_outward, 0)
    lax.fori_loop(half, nc, sweep_crossed, 0)


def rwkv_scan(r, k, v, a, lwf, lwb, k_k, k_a, r_k, ln_w, ln_b, *, chunk):
    b, t, d = r.shape
    w = 2 * R_HEAD_DIM
    pairs = d // w
    nc = t // chunk
    blk = pl.BlockSpec((1, t, w), lambda i, h: (i, 0, h))
    vec = pl.BlockSpec((1, w), lambda i, h: (0, h))
    row = lambda z: z.reshape(1, d)
    return pl.pallas_call(
        functools.partial(_rwkv_scan_kernel, seq=t, chunk=chunk),
        grid=(b, pairs),
        in_specs=[blk] * 6 + [vec] * 5,
        out_specs=blk,
        out_shape=jax.ShapeDtypeStruct((b, t, d), F32),
        scratch_shapes=[pltpu.VMEM((2, t, w), F32),
                        pltpu.VMEM((2, t, w), MXU_DTYPE),
                        pltpu.VMEM((2, nc, w, w), MXU_DTYPE),
                        pltpu.VMEM((2, nc, w, w), F32),
                        pltpu.VMEM((2, nc, w), F32),
                        pltpu.VMEM((2, w, w), F32)],
        compiler_params=_params("parallel", "parallel"),
        name="rwkv_scan",
    )(r, k, v, a, lwf, lwb, row(k_k), row(k_a), row(r_k), row(ln_w), row(ln_b))


def _xattn_kernel(x_ref, nrm_ref, wq_ref, k_ref, v_ref, wo_ref, o_ref):
    x = x_ref[0]
    d = x.shape[-1]
    dh = d // X_HEADS
    q = _mx(_dot(_rms(x, nrm_ref[...]), wq_ref[...]))
    heads = []
    for hd in range(X_HEADS):
        cols = slice(hd * dh, (hd + 1) * dh)
        s = _dot_nt(q[:, cols], k_ref[0, :, cols]) * (dh ** -0.5)
        p = jnp.exp(s - jnp.max(s, axis=-1, keepdims=True))
        p = p / jnp.sum(p, axis=-1, keepdims=True)
        heads.append(_mx(_dot(p, v_ref[0, :, cols])))
    o_ref[0] = x + _dot(jnp.concatenate(heads, axis=1), wo_ref[...])


def cross_attention(x, kv, norm, wq, wo, *, tm):
    b, t, d = x.shape
    m = kv.shape[1]
    tile = pl.BlockSpec((1, tm, d), lambda i, j: (i, j, 0))
    return pl.pallas_call(
        _xattn_kernel,
        grid=(b, t // tm),
        in_specs=[tile, _resident((1, d)), _resident((d, d)),
                  pl.BlockSpec((1, m, d), lambda i, j: (i, 0, 0)),
                  pl.BlockSpec((1, m, d), lambda i, j: (i, 0, 1)),
                  _resident((d, d))],
        out_specs=tile,
        out_shape=jax.ShapeDtypeStruct((b, t, d), F32),
        compiler_params=_params("parallel", "parallel"),
        name="cross_attention",
    )(x, norm.reshape(1, d), _mx(wq), kv, kv, _mx(wo))


def _ffn_kernel(x_ref, nrm_ref, wg_ref, wu_ref, wd_ref, fin_ref, o_ref, *, final_norm):
    x = x_ref[...]
    h = _mx(_rms(x, nrm_ref[...]))
    act = _silu(_dot(h, wg_ref[...])) * _dot(h, wu_ref[...])
    y = x + _dot(act, wd_ref[...])
    o_ref[...] = _rms(y, fin_ref[...]) if final_norm else y


def swiglu_ffn(x2d, norm, w_gate, w_up, w_down, final_gain, *, tm, final_norm):
    n, d = x2d.shape
    dff = w_gate.shape[1]
    row = pl.BlockSpec((tm, d), lambda i: (i, 0))
    return pl.pallas_call(
        functools.partial(_ffn_kernel, final_norm=final_norm),
        grid=(n // tm,),
        in_specs=[row, _resident((1, d)), _resident((d, dff)), _resident((d, dff)), _resident((dff, d)),
                  _resident((1, d))],
        out_specs=row,
        out_shape=jax.ShapeDtypeStruct((n, d), F32),
        compiler_params=_params("parallel"),
        name="swiglu_ffn",
    )(x2d, norm.reshape(1, d), _mx(w_gate), _mx(w_up), _mx(w_down), final_gain.reshape(1, d))


def _pack_even_in(w_in):
    d = w_in.shape[0]
    mw, gkw, gvw = M_HEADS * M_HEAD_DIM, G_HEADS * G_KEY_DIM, G_HEADS * G_VAL_DIM
    widths = (mw, mw, mw, mw, 2 * M_HEADS, 2 * M_HEADS, gkw, gkw, gvw, gvw, 2 * G_DECAY_RANK)
    starts = [0]
    for wdt in widths:
        starts.append(starts[-1] + wdt)
    mq, mk, mv, mo, mi, mf, gq, gk, gv, gg, glr = (w_in[:, starts[j]:starts[j + 1]] for j in range(len(widths)))
    gate = jnp.concatenate([mi, mf, glr], axis=1)
    gate = jnp.pad(gate, ((0, 0), (0, LANES - gate.shape[1])))
    packed = jnp.concatenate([mq, mk, mv, mo, gq, gk, gv, gg, gate], axis=1)
    assert packed.shape == (d, _EV_BLOCKS * LANES)
    return _mx(packed)


def _even_mixer_layer(x, norm, w_in, conv_qk, ig_bias, fg_bias, m_norm, decay_w2, decay_b, g_norm, w_out, *, tm):
    b, t, d = x.shape
    x2d = x.reshape(b * t, d)
    p = rms_matmul(x2d, norm, _pack_even_in(w_in), tm=tm, out_dtype=F32).reshape(b, t, -1)
    gate_bias = jnp.concatenate([ig_bias.reshape(-1), fg_bias.reshape(-1)])
    gate_bias = jnp.pad(gate_bias, (0, LANES - gate_bias.shape[0])).reshape(1, LANES)
    hm = mlstm_mixer(p, conv_qk, gate_bias, m_norm, chunk=min(MLSTM_CHUNK, t))
    og = gla_mixer(p, decay_w2, decay_b, g_norm, chunk=min(GLA_CHUNK, t))
    x2d = even_out_proj(x2d, hm.reshape(b * t, -1), og.reshape(b * t, -1), w_out, tm=tm)
    return x2d.reshape(b, t, d)


def _rwkv_mixer_layer(x, norm, mu, w_rkv, w0, w1, w2, a0, a1, a2, g1, g2, k_k, k_a, r_k, ln_w, ln_b, w_o, *, tm):
    b, t, d = x.shape
    r, k, v, a, lwf, lwb, g = rwkv_in(x, norm, mu, w_rkv, w0, w1, w2, a0, a1, a2, g1, g2, tm=min(tm, 256))
    y = rwkv_scan(r, k, v, a, lwf, lwb, k_k, k_a, r_k.reshape(-1), ln_w, ln_b, chunk=min(RWKV_CHUNK, t))
    x2d = gated_out_proj(x.reshape(b * t, d), y.reshape(b * t, d), g.reshape(b * t, d), w_o, tm=tm)
    return x2d.reshape(b, t, d)


def kernel(x, mem, norm_mix, norm_xattn, norm_mem, norm_ffn, norm_final, xa_wq, xa_wkv, xa_wo, ffn_w_gate, ffn_w_up, ffn_w_down, ev_w_in, ev_conv_qk, ev_m_ig_bias, ev_m_fg_bias, ev_m_norm, ev_g_decay_w2, ev_g_decay_b, ev_g_norm, ev_w_out, od_mu, od_w_rkv, od_w0, od_w1, od_w2, od_a0, od_a1, od_a2, od_g1, od_g2, od_k_k, od_k_a, od_r_k, od_ln_w, od_ln_b, od_w_o):
    b, t, d = x.shape
    m = mem.shape[1]
    depth = norm_mix.shape[0]
    tm = min(512, t)
    for layer in range(depth):
        if layer % 2 == 0:
            e = layer // 2
            x = _even_mixer_layer(x, norm_mix[layer], ev_w_in[e], ev_conv_qk[e], ev_m_ig_bias[e], ev_m_fg_bias[e],
                                  ev_m_norm[e], ev_g_decay_w2[e], ev_g_decay_b[e], ev_g_norm[e], ev_w_out[e], tm=tm)
        else:
            o = layer // 2
            x = _rwkv_mixer_layer(x, norm_mix[layer], od_mu[o], od_w_rkv[o], od_w0[o], od_w1[o], od_w2[o], od_a0[o],
                                  od_a1[o], od_a2[o], od_g1[o], od_g2[o], od_k_k[o], od_k_a[o], od_r_k[o],
                                  od_ln_w[o], od_ln_b[o], od_w_o[o], tm=tm)
        kv = rms_matmul(mem.reshape(b * m, d), norm_mem[layer], _mx(xa_wkv[layer]), tm=min(512, b * m),
                        out_dtype=MXU_DTYPE).reshape(b, m, 2 * d)
        x = cross_attention(x, kv, norm_xattn[layer], xa_wq[layer], xa_wo[layer], tm=tm)
        last = layer == depth - 1
        x = swiglu_ffn(x.reshape(b * t, d), norm_ffn[layer], ffn_w_gate[layer], ffn_w_up[layer], ffn_w_down[layer],
                       norm_final, tm=tm, final_norm=last).reshape(b, t, d)
    return x
```

```python
import functools
import math

import jax
import jax.numpy as jnp
from jax import lax
from jax.experimental import pallas as pl
from jax.experimental.pallas import tpu as pltpu

F32 = jnp.float32
MXU_DTYPE = jnp.bfloat16
LANES = 128
SUBLANES = 8
VMEM_LIMIT_BYTES = 60 * 1024 * 1024
NEG_BIG = -1e30

EPS = 1e-6
CONV_WIDTH = 3
M_HEADS = 4
M_HEAD_DIM = 128
G_HEADS = 4
G_KEY_DIM = 64
G_VAL_DIM = 128
G_DECAY_RANK = 16
G_DECAY_TAU = 16.0
R_HEAD_DIM = 64
R_GN_EPS = 64e-5
X_HEADS = 4

MLSTM_CHUNK = 256
GLA_CHUNK = 64
RWKV_CHUNK = 64
MLSTM_GROUP_CHUNKS = 2
GLA_GROUP_CHUNKS = 4
RWKV_BUILD_CHUNKS = 4

_EV_MQ, _EV_MK, _EV_MV, _EV_MO = 0, 4, 8, 12
_EV_GQ, _EV_GK, _EV_GV, _EV_GG, _EV_GATE = 16, 18, 20, 24, 28
_EV_BLOCKS = 29
_GATE_IG, _GATE_FG, _GATE_LR = 0, 8, 16


def _mx(a):
    return a.astype(MXU_DTYPE)


def _dot(a, b):
    return jnp.dot(_mx(a), _mx(b), preferred_element_type=F32)


def _dot_nt(a, b):
    return lax.dot_general(_mx(a), _mx(b), (((1,), (1,)), ((), ())), preferred_element_type=F32)


def _dot_tn(a, b):
    return lax.dot_general(_mx(a), _mx(b), (((0,), (0,)), ((), ())), preferred_element_type=F32)


def _scan_cumsum(mask, x):
    m = _mx(mask)
    hi = _mx(x)
    rest = x - hi.astype(F32)
    mid = _mx(rest)
    lo = _mx(rest - mid.astype(F32))
    dot = lambda t: jnp.dot(m, t, preferred_element_type=F32)
    return dot(hi) + dot(mid) + dot(lo)


def _rms(x, g):
    return x * lax.rsqrt(jnp.mean(x * x, axis=-1, keepdims=True) + EPS) * g


def _log_sigmoid(x):
    return jnp.minimum(x, 0.0) - jnp.log1p(jnp.exp(-jnp.abs(x)))


def _silu(x):
    return x * jax.nn.sigmoid(x)


def _params(*sem):
    return pltpu.CompilerParams(dimension_semantics=sem, vmem_limit_bytes=VMEM_LIMIT_BYTES)


def _resident(shape):
    nd = len(shape)
    return pl.BlockSpec(shape, lambda *_: (0,) * nd, pipeline_mode=pl.Buffered(1))


def _scan_masks(n, reverse):
    row = lax.broadcasted_iota(jnp.int32, (n, n), 0)
    col = lax.broadcasted_iota(jnp.int32, (n, n), 1)
    if reverse:
        return col >= row, col > row
    return col <= row, col < row


def _rms_matmul_kernel(x_ref, g_ref, w_ref, o_ref):
    h = _rms(x_ref[...], g_ref[...])
    o_ref[...] = _dot(h, w_ref[...]).astype(o_ref.dtype)


def rms_matmul(x2d, g, w, *, tm, out_dtype):
    n, d = x2d.shape
    nout = w.shape[1]
    return pl.pallas_call(
        _rms_matmul_kernel,
        grid=(n // tm,),
        in_specs=[pl.BlockSpec((tm, d), lambda i: (i, 0)), _resident((1, d)), _resident((d, nout))],
        out_specs=pl.BlockSpec((tm, nout), lambda i: (i, 0)),
        out_shape=jax.ShapeDtypeStruct((n, nout), out_dtype),
        compiler_params=_params("parallel"),
        name="rms_matmul",
    )(x2d, g.reshape(1, d), w)


def _extract_col(x, lane_idx):
    lane = lax.broadcasted_iota(jnp.int32, x.shape, 1)
    return jnp.sum(jnp.where(lane == lane_idx, x, 0.0), axis=1, keepdims=True)


def _mlstm_kernel(q_ref, k_ref, v_ref, o_ref, gate_ref, conv_ref, gbias_ref, norm_ref, out_ref,
                  qc_ref, kc_ref, gl_ref, hf_ref, hb_ref, c_ref, m_ref, *, seq, chunk):
    L = chunk
    nc = seq // L
    d = M_HEAD_DIM
    head = pl.program_id(1)

    row_id = lax.broadcasted_iota(jnp.int32, (L, d), 0)
    lane_id = lax.broadcasted_iota(jnp.int32, (L, LANES), 1)
    is_fg = (lane_id >= _GATE_FG) & (lane_id < _GATE_LR)

    def conv_silu(src_ref, w, r0, c):
        x = src_ref[0, pl.ds(r0, L), :]
        prev_row = src_ref[0, pl.ds(jnp.maximum(r0 - 1, 0), 1), :] * jnp.where(c > 0, 1.0, 0.0)
        next_row = src_ref[0, pl.ds(jnp.minimum(r0 + L, seq - 1), 1), :] * jnp.where(c < nc - 1, 1.0, 0.0)
        x_prev = jnp.where(row_id == 0, prev_row, pltpu.roll(x, 1, axis=0))
        x_next = jnp.where(row_id == L - 1, next_row, pltpu.roll(x, L - 1, axis=0))
        return _silu(x_prev * w[0:1, :] + x * w[1:2, :] + x_next * w[2:3, :])

    def prologue(c, carry):
        r0 = pl.multiple_of(c * L, L)
        qc_ref[pl.ds(r0, L), :] = (conv_silu(q_ref, conv_ref[0], r0, c) * (d ** -0.5)).astype(qc_ref.dtype)
        kc_ref[pl.ds(r0, L), :] = conv_silu(k_ref, conv_ref[1], r0, c)
        g = gate_ref[0, pl.ds(r0, L), :] + gbias_ref[...]
        gl_ref[pl.ds(r0, L), :] = jnp.where(is_fg, _log_sigmoid(g), g)
        return carry

    lax.fori_loop(0, nc, prologue, 0)

    c_ref[...] = jnp.zeros_like(c_ref)
    m_ref[...] = jnp.zeros_like(m_ref)
    ones = jnp.ones((L, d), MXU_DTYPE)

    masks = [_scan_masks(L, reverse)[0] for reverse in (False, True)]
    dst_refs = (hf_ref, hb_ref)
    cb = MLSTM_GROUP_CHUNKS
    lane_sel = lax.broadcasted_iota(jnp.int32, (L, LANES), 1)

    def split_lanes(col, first):
        hi = _mx(col).astype(F32)
        mid = _mx(col - hi).astype(F32)
        lo = col - hi - mid
        out = jnp.where(lane_sel < 6, 1.0, 0.0)
        for offset, term in enumerate((hi, mid, lo)):
            out = jnp.where(lane_sel == first + offset, term, out)
        return _mx(out)

    def body(i, carry):
        chains = [(dr, j) for j in range(cb) for dr in range(2)]
        chunk_of = lambda dr, j: (i * cb + j) if dr == 0 else (nc - 1 - (i * cb + j))
        rows = [pl.ds(pl.multiple_of(chunk_of(dr, j) * L, L), L) for dr, j in chains]
        each = lambda f, *lists: [f(*args) for args in zip(*lists)]

        g = each(lambda r_: gl_ref[r_, :], rows)
        bcum = each(lambda dj, g_: _scan_cumsum(masks[dj[0]], g_), chains, g)
        ig_col = each(lambda dj, g_: _extract_col(g_, _GATE_IG + dj[0] * M_HEADS + head), chains, g)
        b_col = each(lambda dj, b_: _extract_col(b_, _GATE_FG + dj[0] * M_HEADS + head), chains, bcum)
        g_tot = each(lambda dj, b_: b_[0:1, :] if dj[0] == 1 else b_[L - 1:L, :], chains, b_col)
        dmat = each(lambda dj, b_, i_: jnp.where(masks[dj[0]], _dot_nt(split_lanes(b_, 0), split_lanes(i_ - b_, 3)),
                                                 NEG_BIG), chains, b_col, ig_col)
        dmax = each(lambda d_: jnp.max(d_, axis=1, keepdims=True), dmat)
        q = each(lambda r_: qc_ref[r_, :], rows)
        k = each(lambda r_: kc_ref[r_, :], rows)
        v_aug = each(lambda r_: jnp.concatenate([_mx(v_ref[0, r_, :]), ones], axis=1), rows)
        sc = each(lambda q_, k_, d_, m_: _mx(_dot_nt(q_, k_) * jnp.exp(d_ - m_)), q, k, dmat, dmax)
        nd_intra = each(_dot, sc, v_aug)
        kw_log = each(lambda g_, b_, i_: g_ - b_ + i_, g_tot, b_col, ig_col)
        kw_max = each(lambda w_: jnp.max(w_, axis=0, keepdims=True), kw_log)
        d_c = each(lambda k_, w_, a_, v_: _dot_tn(_mx(k_ * jnp.exp(w_ - a_)), v_), k, kw_log, kw_max, v_aug)

        c_start = [None] * len(chains)
        m_start = [None] * len(chains)
        for dr in range(2):
            c_cur = c_ref[dr]
            m_cur = m_ref[dr][:, 0:1]
            for j in range(cb):
                idx = chains.index((dr, j))
                c_start[idx], m_start[idx] = _mx(c_cur), m_cur
                m_new = jnp.maximum(g_tot[idx] + m_cur, kw_max[idx])
                c_cur = jnp.exp(g_tot[idx] + m_cur - m_new) * c_cur + jnp.exp(kw_max[idx] - m_new) * d_c[idx]
                m_cur = m_new
            c_ref[dr] = c_cur
            m_ref[dr] = jnp.broadcast_to(m_cur, (1, LANES))

        nd_inter = each(_dot, q, c_start)
        for idx, (dr, j) in enumerate(chains):
            inter_log = b_col[idx] + m_start[idx]
            m_t = jnp.maximum(inter_log, dmax[idx])
            numden = jnp.exp(dmax[idx] - m_t) * nd_intra[idx] + jnp.exp(inter_log - m_t) * nd_inter[idx]
            dst_refs[dr][rows[idx], :] = numden[:, :d] / jnp.maximum(jnp.abs(numden[:, d:]), jnp.exp(-m_t))
        return carry

    lax.fori_loop(0, nc // cb, body, 0)

    def epilogue(c, carry):
        r0 = pl.multiple_of(c * L, L)
        h = hf_ref[pl.ds(r0, L), :] + hb_ref[pl.ds(r0, L), :]
        h = _rms(h, norm_ref[...])
        out_ref[0, pl.ds(r0, L), :] = h * jax.nn.sigmoid(o_ref[0, pl.ds(r0, L), :])
        return carry

    lax.fori_loop(0, nc, epilogue, 0)


def mlstm_mixer(p, conv_qk, gate_bias, m_norm, *, chunk):
    b, t, _ = p.shape
    d = M_HEAD_DIM
    blk = lambda off: pl.BlockSpec((1, t, d), lambda i, h, off=off: (i, 0, off + h))
    conv = conv_qk.reshape(CONV_WIDTH, 2, M_HEADS, d).transpose(2, 1, 0, 3)
    return pl.pallas_call(
        functools.partial(_mlstm_kernel, seq=t, chunk=chunk),
        grid=(b, M_HEADS),
        in_specs=[blk(_EV_MQ), blk(_EV_MK), blk(_EV_MV), blk(_EV_MO),
                  pl.BlockSpec((1, t, LANES), lambda i, h: (i, 0, _EV_GATE)),
                  pl.BlockSpec((None, 2, CONV_WIDTH, d), lambda i, h: (h, 0, 0, 0)),
                  pl.BlockSpec((1, LANES), lambda i, h: (0, 0)),
                  pl.BlockSpec((1, d), lambda i, h: (0, h))],
        out_specs=pl.BlockSpec((1, t, d), lambda i, h: (i, 0, h)),
        out_shape=jax.ShapeDtypeStruct((b, t, M_HEADS * d), F32),
        scratch_shapes=[pltpu.VMEM((t, d), MXU_DTYPE),
                        pltpu.VMEM((t, d), F32),
                        pltpu.VMEM((t, LANES), F32),
                        pltpu.VMEM((t, d), F32),
                        pltpu.VMEM((t, d), F32),
                        pltpu.VMEM((2, d, 2 * d), F32),
                        pltpu.VMEM((2, 1, LANES), F32)],
        compiler_params=_params("parallel", "parallel"),
        name="mlstm",
    )(p, p, p, p, p, conv, gate_bias, m_norm.reshape(1, M_HEADS * d))


def _gla_kernel(q_ref, k_ref, v0_ref, v1_ref, g0_ref, g1_ref, gate_ref, w2_ref, b2_ref, norm_ref, out_ref,
                of_ref, ob_ref, s_ref, *, seq, chunk):
    L = chunk
    nc = seq // L
    dk2 = 2 * G_KEY_DIM
    dv = G_VAL_DIM
    lane_k = lax.broadcasted_iota(jnp.int32, (L, dk2), 1)
    head0 = (lane_k < G_KEY_DIM).astype(F32)
    head1 = 1.0 - head0
    s_row = lax.broadcasted_iota(jnp.int32, (2 * dv, dk2), 0)
    s_col = lax.broadcasted_iota(jnp.int32, (2 * dv, dk2), 1)
    s_mask = ((s_row < dv) == (s_col < G_KEY_DIM)).astype(F32)

    s_ref[...] = jnp.zeros_like(s_ref)
    masks = [_scan_masks(L, reverse)[0] for reverse in (False, True)]
    masks2 = [jnp.concatenate([m, m], axis=0) for m in masks]
    dst_refs = (of_ref, ob_ref)
    cb = GLA_GROUP_CHUNKS

    def body(i, carry):
        chains = [(d, j) for j in range(cb) for d in range(2)]
        chunk_of = lambda d, j: (i * cb + j) if d == 0 else (nc - 1 - (i * cb + j))
        rows = [pl.ds(pl.multiple_of(chunk_of(d, j) * L, L), L) for d, j in chains]
        each = lambda f, *lists: [f(*args) for args in zip(*lists)]

        z = each(lambda dj, r_: _dot(gate_ref[0, r_, :], w2_ref[dj[0], 0]) + b2_ref[dj[0], 0], chains, rows)
        la = each(lambda z_: _log_sigmoid(z_) * (1.0 / G_DECAY_TAU), z)
        bcum = each(lambda dj, l_: _scan_cumsum(masks[dj[0]], l_), chains, la)
        g_tot = each(lambda dj, b_: b_[0:1, :] if dj[0] == 1 else b_[L - 1:L, :], chains, bcum)
        mid = each(lambda b_: b_[L // 2:L // 2 + 1, :], bcum)
        q = each(lambda r_: q_ref[0, r_, :] * (G_KEY_DIM ** -0.5), rows)
        k = each(lambda r_: k_ref[0, r_, :], rows)
        v = each(lambda r_: jnp.concatenate([_mx(v0_ref[0, r_, :]), _mx(v1_ref[0, r_, :])], axis=1), rows)
        q_in = each(lambda q_, b_, m_: q_ * jnp.exp(b_ - m_), q, bcum, mid)
        q_in2 = each(lambda q_: _mx(jnp.concatenate([q_ * head0, q_ * head1], axis=0)), q_in)
        k_in = each(lambda k_, b_, m_: _mx(k_ * jnp.exp(m_ - b_)), k, bcum, mid)
        scores = each(lambda dj, q_, k_: _mx(jnp.where(masks2[dj[0]], _dot_nt(q_, k_), 0.0)), chains, q_in2, k_in)
        o_intra = each(lambda a_, v_: jnp.concatenate([_dot(a_[:L], v_[:, :dv]), _dot(a_[L:], v_[:, dv:])], axis=1),
                       scores, v)
        q_hat = each(lambda q_, b_: _mx(q_ * jnp.exp(b_)), q, bcum)
        k_hat = each(lambda k_, b_, g_: _mx(k_ * jnp.exp(g_ - b_)), k, bcum, g_tot)
        d_s = each(lambda v_, k_: _dot_tn(v_, k_) * s_mask, v, k_hat)
        decay = each(jnp.exp, g_tot)

        s_at_start = [None] * len(chains)
        for d in range(2):
            s_cur = s_ref[d]
            for j in range(cb):
                idx = chains.index((d, j))
                s_at_start[idx] = _mx(s_cur)
                s_cur = s_cur * decay[idx] + d_s[idx]
            s_ref[d] = s_cur
        o_inter = each(_dot_nt, q_hat, s_at_start)
        for idx, (d, j) in enumerate(chains):
            dst_refs[d][rows[idx], :] = o_intra[idx] + o_inter[idx]
        return carry

    lax.fori_loop(0, nc // cb, body, 0)

    def epilogue(c, carry):
        r0 = pl.multiple_of(c * L, L)
        o = of_ref[pl.ds(r0, L), :] + ob_ref[pl.ds(r0, L), :]
        g = jnp.concatenate([g0_ref[0, pl.ds(r0, L), :], g1_ref[0, pl.ds(r0, L), :]], axis=1)
        nrm = norm_ref[...]
        o = jnp.concatenate([_rms(o[:, :dv], nrm[:, :dv]), _rms(o[:, dv:], nrm[:, dv:])], axis=1)
        out_ref[0, pl.ds(r0, L), :] = o * _silu(g)
        return carry

    lax.fori_loop(0, nc, epilogue, 0)


def gla_mixer(p, decay_w2, decay_b, g_norm, *, chunk):
    b, t, _ = p.shape
    dv = G_VAL_DIM
    pairs = G_HEADS // 2
    dk2 = 2 * G_KEY_DIM
    w2 = jnp.zeros((2, LANES, pairs * dk2), F32)
    for direction in range(2):
        lo = _GATE_LR + direction * G_DECAY_RANK
        w2 = w2.at[direction, lo:lo + G_DECAY_RANK].set(decay_w2[direction])
    w2 = _mx(w2.reshape(2, LANES, pairs, dk2).transpose(0, 2, 1, 3))
    b2 = decay_b.reshape(2, pairs, 1, dk2)
    blk = lambda off, mult: pl.BlockSpec((1, t, LANES), lambda i, h, off=off, mult=mult: (i, 0, off + mult * h))
    return pl.pallas_call(
        functools.partial(_gla_kernel, seq=t, chunk=chunk),
        grid=(b, pairs),
        in_specs=[blk(_EV_GQ, 1), blk(_EV_GK, 1), blk(_EV_GV, 2), blk(_EV_GV + 1, 2),
                  blk(_EV_GG, 2), blk(_EV_GG + 1, 2),
                  pl.BlockSpec((1, t, LANES), lambda i, h: (i, 0, _EV_GATE)),
                  pl.BlockSpec((2, 1, LANES, dk2), lambda i, h: (0, h, 0, 0)),
                  pl.BlockSpec((2, 1, 1, dk2), lambda i, h: (0, h, 0, 0)),
                  pl.BlockSpec((1, 2 * dv), lambda i, h: (0, h))],
        out_specs=pl.BlockSpec((1, t, 2 * dv), lambda i, h: (i, 0, h)),
        out_shape=jax.ShapeDtypeStruct((b, t, G_HEADS * dv), F32),
        scratch_shapes=[pltpu.VMEM((t, 2 * dv), F32),
                        pltpu.VMEM((t, 2 * dv), F32),
                        pltpu.VMEM((2, 2 * dv, dk2), F32)],
        compiler_params=_params("parallel", "parallel"),
        name="gla",
    )(p, p, p, p, p, p, p, w2, b2, g_norm.reshape(1, G_HEADS * dv))


def _even_out_kernel(x_ref, hm_ref, og_ref, wm_ref, wg_ref, o_ref):
    o_ref[...] = x_ref[...] + _dot(hm_ref[...], wm_ref[...]) + _dot(og_ref[...], wg_ref[...])


def even_out_proj(x2d, hm, og, w_out, *, tm):
    n, d = x2d.shape
    wm, wg = _mx(w_out[:hm.shape[1]]), _mx(w_out[hm.shape[1]:])
    row = lambda width: pl.BlockSpec((tm, width), lambda i: (i, 0))
    return pl.pallas_call(
        _even_out_kernel,
        grid=(n // tm,),
        in_specs=[row(d), row(hm.shape[1]), row(og.shape[1]), _resident(wm.shape), _resident(wg.shape)],
        out_specs=row(d),
        out_shape=jax.ShapeDtypeStruct((n, d), F32),
        compiler_params=_params("parallel"),
        name="even_out_proj",
    )(x2d, hm, og, wm, wg)


def _gated_out_kernel(x_ref, y_ref, g_ref, w_ref, o_ref):
    o_ref[...] = x_ref[...] + _dot(y_ref[...] * g_ref[...], w_ref[...])


def gated_out_proj(x2d, y, g, w, *, tm):
    n, d = x2d.shape
    row = pl.BlockSpec((tm, d), lambda i: (i, 0))
    return pl.pallas_call(
        _gated_out_kernel,
        grid=(n // tm,),
        in_specs=[row, row, row, _resident(w.shape)],
        out_specs=row,
        out_shape=jax.ShapeDtypeStruct((n, d), F32),
        compiler_params=_params("parallel"),
        name="gated_out_proj",
    )(x2d, y, g, _mx(w))


def _rwkv_in_kernel(x_ref, xp_ref, xn_ref, nrm_ref, mu_ref, wrkv_ref, w1_ref, w2_ref, w0_ref,
                    a1_ref, a2_ref, a0_ref, g1_ref, g2_ref,
                    r_ref, k_ref, v_ref, a_ref, lwf_ref, lwb_ref, g_ref, *, tm):
    i = pl.program_id(1)
    nt = pl.num_programs(1)
    nrm = nrm_ref[...]
    h = _rms(x_ref[0], nrm)
    before = _rms(xp_ref[0, 0], nrm)[SUBLANES - 1:SUBLANES, :] * jnp.where(i > 0, 1.0, 0.0)
    after = _rms(xn_ref[0, 0], nrm)[0:1, :] * jnp.where(i < nt - 1, 1.0, 0.0)
    row = lax.broadcasted_iota(jnp.int32, h.shape, 0)
    h_prev = jnp.where(row == 0, before, pltpu.roll(h, 1, axis=0))
    h_next = jnp.where(row == tm - 1, after, pltpu.roll(h, tm - 1, axis=0))
    hh = 0.5 * (h_prev + h_next) - h
    mix = lambda j: h + hh * mu_ref[j:j + 1, :]
    xr, xw, xk, xv, xa, xg = (mix(j) for j in range(6))
    r_ref[0] = _dot(xr, wrkv_ref[0])
    k_ref[0] = _dot(xk, wrkv_ref[1])
    v_ref[0] = _dot(xv, wrkv_ref[2])
    a_ref[0] = jax.nn.sigmoid(a0_ref[...] + _dot(_dot(xa, a1_ref[...]), a2_ref[...]))
    g_ref[0] = _dot(jax.nn.sigmoid(_dot(xg, g1_ref[...])), g2_ref[...])
    for direction, dst in ((0, lwf_ref), (1, lwb_ref)):
        z = w0_ref[direction:direction + 1, :] + _dot(jnp.tanh(_dot(xw, w1_ref[direction])), w2_ref[direction])
        dst[0] = jax.nn.sigmoid(z) * -math.exp(-0.5)


def rwkv_in(x, norm, mu, w_rkv, w0, w1, w2, a0, a1, a2, g1, g2, *, tm):
    b, t, d = x.shape
    x8 = x.reshape(b, t // SUBLANES, SUBLANES, d)
    tb = tm // SUBLANES
    nb = t // SUBLANES
    rank_g = g1.shape[1]
    pad_g = -rank_g % LANES
    g1p = jnp.pad(g1, ((0, 0), (0, pad_g)))
    g2p = jnp.pad(g2, ((0, pad_g), (0, 0)))
    tile = pl.BlockSpec((1, tm, d), lambda i, j: (i, j, 0))
    out = jax.ShapeDtypeStruct((b, t, d), F32)
    weights = [norm.reshape(1, d), mu, _mx(w_rkv), _mx(w1), _mx(w2), w0,
               _mx(a1), _mx(a2), a0.reshape(1, d), _mx(g1p), _mx(g2p)]
    return pl.pallas_call(
        functools.partial(_rwkv_in_kernel, tm=tm),
        grid=(b, t // tm),
        in_specs=[tile,
                  pl.BlockSpec((1, 1, SUBLANES, d), lambda i, j: (i, jnp.maximum(j * tb - 1, 0), 0, 0)),
                  pl.BlockSpec((1, 1, SUBLANES, d), lambda i, j: (i, jnp.minimum((j + 1) * tb, nb - 1), 0, 0))]
                 + [_resident(w.shape) for w in weights],
        out_specs=[tile] * 7,
        out_shape=[out] * 7,
        compiler_params=_params("parallel", "parallel"),
        name="rwkv_in",
    )(x, x8, x8, *weights)


def _rwkv_scan_kernel(r_ref, k_ref, v_ref, a_ref, lwf_ref, lwb_ref, kk_ref, ka_ref, rk_ref, lnw_ref, lnb_ref,
                      out_ref, y_ref, rq_ref, m_ref, sadd_ref, gam_ref, s_ref, *, seq, chunk):
    L = chunk
    nc = seq // L
    n = R_HEAD_DIM
    lane = lax.broadcasted_iota(jnp.int32, (L, 2 * n), 1)
    head0 = (lane < n).astype(F32)
    head1 = 1.0 - head0

    def head_sum(x):
        s0 = jnp.sum(x * head0, axis=1, keepdims=True)
        s1 = jnp.sum(x * head1, axis=1, keepdims=True)
        return s0 * head0 + s1 * head1

    def stack(x):
        return jnp.concatenate([x * head0, x * head1], axis=0)

    def load(c):
        r0 = pl.multiple_of(c * L, L)
        rows = pl.ds(r0, L)
        r, k, v, a = r_ref[0, rows, :], k_ref[0, rows, :], v_ref[0, rows, :], a_ref[0, rows, :]
        kk = k * kk_ref[...]
        kk = kk / jnp.maximum(jnp.sqrt(head_sum(kk * kk)), 1e-12)
        kp = k * (1.0 + (a - 1.0) * ka_ref[...])
        return rows, r, kp, v, -kk, kk * a

    def unstack(x):
        return x[:L] + x[L:]

    masks_l = [_scan_masks(L, reverse)[0] for reverse in (False, True)]
    masks = [_scan_masks(2 * L, reverse) for reverse in (False, True)]
    lw_refs = (lwf_ref, lwb_ref)
    cb = min(RWKV_BUILD_CHUNKS, nc // 2)
    groups = nc // cb
    chunk_of = lambda d, g, j: (g * cb + j) if d == 0 else (nc - 1 - (g * cb + j))
    each = lambda f, *lists: [f(*args) for args in zip(*lists)]

    def finish(c, y):
        rows = pl.ds(pl.multiple_of(c * L, L), L)
        r, k, v, a = r_ref[0, rows, :], k_ref[0, rows, :], v_ref[0, rows, :], a_ref[0, rows, :]
        kp = k * (1.0 + (a - 1.0) * ka_ref[...])
        mean = head_sum(y) * (1.0 / n)
        yc = y - mean
        var = head_sum(yc * yc) * (1.0 / n)
        yn = yc * lax.rsqrt(var + R_GN_EPS) * lnw_ref[...] + lnb_ref[...]
        out_ref[0, rows, :] = yn + head_sum(r * kp * rk_ref[...]) * v

    def sweep_steps(g, crossed):
        state = [s_ref[d] for d in range(2)]

        def step(j):
            cs = [chunk_of(d, g, j) for d in range(2)]
            rows = [pl.ds(pl.multiple_of(c * L, L), L) for c in cs]
            s_mx = each(_mx, state)
            y = [y_ref[d, rows[d], :] + _dot_nt(rq_ref[d, rows[d], :], s_mx[d]) for d in range(2)]
            for d in range(2):
                state[d] = (state[d] * gam_ref[d, pl.ds(cs[d], 1), :] + _dot(s_mx[d], m_ref[d, cs[d]])
                            + sadd_ref[d, cs[d]])
            if j == cb - 1:
                for d in range(2):
                    s_ref[d] = state[d]
            for d in range(2):
                if crossed:
                    finish(cs[d], y[d] + y_ref[1 - d, rows[d], :])
                else:
                    y_ref[d, rows[d], :] = y[d]

        return [functools.partial(step, j) for j in range(cb)]

    def build_group(g, interleaved=()):
        pending = list(interleaved)
        run_one = lambda: pending.pop(0)() if pending else None
        chains = [(d, j) for j in range(cb) for d in range(2)]
        chunk_ids = [chunk_of(d, g, j) for d, j in chains]
        loaded = [load(c) for c in chunk_ids]

        lw = each(lambda dj, ld: lw_refs[dj[0]][0, ld[0], :], chains, loaded)
        cin = each(lambda dj, x: _scan_cumsum(masks_l[dj[0]], x), chains, lw)
        ctot = each(lambda dj, c_: c_[0:1, :] if dj[0] == 1 else c_[L - 1:L, :], chains, cin)
        e_out = each(lambda c_: jnp.exp(-c_), cin)
        e_end = each(lambda c_, t_: jnp.exp(t_ - c_), cin, ctot)
        a_hat = each(lambda ld, c_, l_: stack(ld[4] * jnp.exp(c_ - l_)), loaded, cin, lw)
        r_hat = each(lambda ld, c_: stack(ld[1] * jnp.exp(c_)), loaded, cin)
        x = each(lambda a_, r_: _mx(jnp.concatenate([a_, r_], axis=0)), a_hat, r_hat)
        yk = each(lambda ld, e_: _mx(jnp.concatenate([stack(ld[5] * e_), stack(ld[2] * e_)], axis=0)), loaded, e_out)
        run_one()
        gram = each(_dot_nt, x, yk)
        a_ab = each(lambda dj, g_: jnp.where(masks[dj[0]][1], g_[:2 * L, :2 * L], 0.0), chains, gram)
        a_ak = each(lambda dj, g_: _mx(jnp.where(masks[dj[0]][1], g_[:2 * L, 2 * L:], 0.0)), chains, gram)
        a_rb = each(lambda dj, g_: _mx(jnp.where(masks[dj[0]][0], g_[2 * L:, :2 * L], 0.0)), chains, gram)
        a_rk = each(lambda dj, g_: _mx(jnp.where(masks[dj[0]][0], g_[2 * L:, 2 * L:], 0.0)), chains, gram)

        eye = (lax.broadcasted_iota(jnp.int32, (2 * L, 2 * L), 0)
               == lax.broadcasted_iota(jnp.int32, (2 * L, 2 * L), 1)).astype(F32)
        t_inv = each(lambda n_: eye + n_, a_ab)
        power = each(lambda n_: _mx(_dot(n_, n_)), a_ab)
        for _ in range(L.bit_length() - 3):
            both = each(lambda p_, t_: _dot(p_, jnp.concatenate([p_, _mx(t_)], axis=1)), power, t_inv)
            power = each(lambda b_: _mx(b_[:, :2 * L]), both)
            t_inv = each(lambda t_, b_: t_ + b_[:, 2 * L:], t_inv, both)
            run_one()
        t_inv = each(lambda p_, t_: t_ + _dot(p_, t_), power, t_inv)

        v2 = each(lambda ld: _mx(stack(ld[3])), loaded)
        av = each(_dot, a_ak, v2)
        w = each(lambda t_, a_, av_: _dot(t_, jnp.concatenate([_mx(a_), _mx(av_)], axis=1)), t_inv, a_hat, av)
        w_mx = each(_mx, w)
        no_v = jnp.zeros((2 * L, 2 * n), MXU_DTYPE)
        z = each(lambda b_, k_, w_, v_: _dot(jnp.concatenate([b_, k_], axis=1),
                                             jnp.concatenate([w_, jnp.concatenate([no_v, v_], axis=1)], axis=0)),
                 a_rb, a_rk, w_mx, v2)
        while pending:
            run_one()
        b_end = each(lambda ld, e_: _mx(stack(ld[5] * e_)), loaded, e_end)
        k_end = each(lambda ld, e_: _mx(stack(ld[2] * e_)), loaded, e_end)
        m_lr = each(lambda w_, b_: _dot_tn(w_[:, :2 * n], b_), w_mx, b_end)
        s_add = each(lambda w_, v_, b_, k_: _dot_tn(jnp.concatenate([w_[:, 2 * n:], v_], axis=0),
                                                    jnp.concatenate([b_, k_], axis=0)), w_mx, v2, b_end, k_end)
        for idx, (d, j) in enumerate(chains):
            rows = loaded[idx][0]
            y_ref[d, rows, :] = unstack(z[idx][:, 2 * n:])
            rq_ref[d, rows, :] = unstack(r_hat[idx] + z[idx][:, :2 * n]).astype(rq_ref.dtype)
            m_ref[d, chunk_ids[idx]] = m_lr[idx].astype(m_ref.dtype)
            sadd_ref[d, chunk_ids[idx]] = s_add[idx]
            gam_ref[d, pl.ds(chunk_ids[idx], 1), :] = jnp.exp(ctot[idx])

    s_ref[...] = jnp.zeros_like(s_ref)
    build_group(0)

    def before_crossing(g, carry):
        build_group(g, sweep_steps(g - 1, crossed=False))
        return carry

    def after_crossing(g, carry):
        build_group(g, sweep_steps(g - 1, crossed=True))
        return carry

    lax.fori_loop(1, groups // 2 + 1, before_crossing, 0)
    lax.fori_loop(groups // 2 + 1, groups, after_crossing, 0)
    for step in sweep_steps(groups - 1, crossed=groups > 1):
        step()


def rwkv_scan(r, k, v, a, lwf, lwb, k_k, k_a, r_k, ln_w, ln_b, *, chunk):
    b, t, d = r.shape
    w = 2 * R_HEAD_DIM
    pairs = d // w
    nc = t // chunk
    blk = pl.BlockSpec((1, t, w), lambda i, h: (i, 0, h))
    vec = pl.BlockSpec((1, w), lambda i, h: (0, h))
    row = lambda z: z.reshape(1, d)
    return pl.pallas_call(
        functools.partial(_rwkv_scan_kernel, seq=t, chunk=chunk),
        grid=(b, pairs),
        in_specs=[blk] * 6 + [vec] * 5,
        out_specs=blk,
        out_shape=jax.ShapeDtypeStruct((b, t, d), F32),
        scratch_shapes=[pltpu.VMEM((2, t, w), F32),
                        pltpu.VMEM((2, t, w), MXU_DTYPE),
                        pltpu.VMEM((2, nc, w, w), MXU_DTYPE),
                        pltpu.VMEM((2, nc, w, w), F32),
                        pltpu.VMEM((2, nc, w), F32),
                        pltpu.VMEM((2, w, w), F32)],
        compiler_params=_params("parallel", "parallel"),
        name="rwkv_scan",
    )(r, k, v, a, lwf, lwb, row(k_k), row(k_a), row(r_k), row(ln_w), row(ln_b))


def _xattn_kernel(x_ref, nrm_ref, wq_ref, k_ref, v_ref, wo_ref, o_ref):
    x = x_ref[0]
    d = x.shape[-1]
    dh = d // X_HEADS
    q = _mx(_dot(_rms(x, nrm_ref[...]), wq_ref[...]))
    cols = [slice(hd * dh, (hd + 1) * dh) for hd in range(X_HEADS)]
    s = [_dot_nt(q[:, c], k_ref[0, :, c]) * (dh ** -0.5) for c in cols]
    p = [jnp.exp(s_ - jnp.max(s_, axis=-1, keepdims=True)) for s_ in s]
    p = [_mx(p_ / jnp.sum(p_, axis=-1, keepdims=True)) for p_ in p]
    heads = [_mx(_dot(p_, v_ref[0, :, c])) for p_, c in zip(p, cols)]
    o_ref[0] = x + _dot(jnp.concatenate(heads, axis=1), wo_ref[...])


def cross_attention(x, kv, norm, wq, wo, *, tm):
    b, t, d = x.shape
    m = kv.shape[1]
    tile = pl.BlockSpec((1, tm, d), lambda i, j: (i, j, 0))
    return pl.pallas_call(
        _xattn_kernel,
        grid=(b, t // tm),
        in_specs=[tile, _resident((1, d)), _resident((d, d)),
                  pl.BlockSpec((1, m, d), lambda i, j: (i, 0, 0)),
                  pl.BlockSpec((1, m, d), lambda i, j: (i, 0, 1)),
                  _resident((d, d))],
        out_specs=tile,
        out_shape=jax.ShapeDtypeStruct((b, t, d), F32),
        compiler_params=_params("parallel", "parallel"),
        name="cross_attention",
    )(x, norm.reshape(1, d), _mx(wq), kv, kv, _mx(wo))


def _ffn_kernel(x_ref, nrm_ref, wg_ref, wu_ref, wd_ref, fin_ref, o_ref, *, final_norm):
    x = x_ref[...]
    h = _mx(_rms(x, nrm_ref[...]))
    act = _silu(_dot(h, wg_ref[...])) * _dot(h, wu_ref[...])
    y = x + _dot(act, wd_ref[...])
    o_ref[...] = _rms(y, fin_ref[...]) if final_norm else y


def swiglu_ffn(x2d, norm, w_gate, w_up, w_down, final_gain, *, tm, final_norm):
    n, d = x2d.shape
    dff = w_gate.shape[1]
    row = pl.BlockSpec((tm, d), lambda i: (i, 0))
    return pl.pallas_call(
        functools.partial(_ffn_kernel, final_norm=final_norm),
        grid=(n // tm,),
        in_specs=[row, _resident((1, d)), _resident((d, dff)), _resident((d, dff)), _resident((dff, d)),
                  _resident((1, d))],
        out_specs=row,
        out_shape=jax.ShapeDtypeStruct((n, d), F32),
        compiler_params=_params("parallel"),
        name="swiglu_ffn",
    )(x2d, norm.reshape(1, d), _mx(w_gate), _mx(w_up), _mx(w_down), final_gain.reshape(1, d))


def _pack_even_in(w_in):
    d = w_in.shape[0]
    mw, gkw, gvw = M_HEADS * M_HEAD_DIM, G_HEADS * G_KEY_DIM, G_HEADS * G_VAL_DIM
    widths = (mw, mw, mw, mw, 2 * M_HEADS, 2 * M_HEADS, gkw, gkw, gvw, gvw, 2 * G_DECAY_RANK)
    starts = [0]
    for wdt in widths:
        starts.append(starts[-1] + wdt)
    mq, mk, mv, mo, mi, mf, gq, gk, gv, gg, glr = (w_in[:, starts[j]:starts[j + 1]] for j in range(len(widths)))
    gate = jnp.concatenate([mi, mf, glr], axis=1)
    gate = jnp.pad(gate, ((0, 0), (0, LANES - gate.shape[1])))
    packed = jnp.concatenate([mq, mk, mv, mo, gq, gk, gv, gg, gate], axis=1)
    assert packed.shape == (d, _EV_BLOCKS * LANES)
    return _mx(packed)


def _even_mixer_layer(x, norm, w_in, conv_qk, ig_bias, fg_bias, m_norm, decay_w2, decay_b, g_norm, w_out, *, tm):
    b, t, d = x.shape
    x2d = x.reshape(b * t, d)
    p = rms_matmul(x2d, norm, _pack_even_in(w_in), tm=tm, out_dtype=F32).reshape(b, t, -1)
    gate_bias = jnp.concatenate([ig_bias.reshape(-1), fg_bias.reshape(-1)])
    gate_bias = jnp.pad(gate_bias, (0, LANES - gate_bias.shape[0])).reshape(1, LANES)
    hm = mlstm_mixer(p, conv_qk, gate_bias, m_norm, chunk=min(MLSTM_CHUNK, t))
    og = gla_mixer(p, decay_w2, decay_b, g_norm, chunk=min(GLA_CHUNK, t))
    x2d = even_out_proj(x2d, hm.reshape(b * t, -1), og.reshape(b * t, -1), w_out, tm=tm)
    return x2d.reshape(b, t, d)


def _rwkv_mixer_layer(x, norm, mu, w_rkv, w0, w1, w2, a0, a1, a2, g1, g2, k_k, k_a, r_k, ln_w, ln_b, w_o, *, tm):
    b, t, d = x.shape
    r, k, v, a, lwf, lwb, g = rwkv_in(x, norm, mu, w_rkv, w0, w1, w2, a0, a1, a2, g1, g2, tm=min(tm, 256))
    y = rwkv_scan(r, k, v, a, lwf, lwb, k_k, k_a, r_k.reshape(-1), ln_w, ln_b, chunk=min(RWKV_CHUNK, t))
    x2d = gated_out_proj(x.reshape(b * t, d), y.reshape(b * t, d), g.reshape(b * t, d), w_o, tm=tm)
    return x2d.reshape(b, t, d)


def kernel(x, mem, norm_mix, norm_xattn, norm_mem, norm_ffn, norm_final, xa_wq, xa_wkv, xa_wo, ffn_w_gate, ffn_w_up, ffn_w_down, ev_w_in, ev_conv_qk, ev_m_ig_bias, ev_m_fg_bias, ev_m_norm, ev_g_decay_w2, ev_g_decay_b, ev_g_norm, ev_w_out, od_mu, od_w_rkv, od_w0, od_w1, od_w2, od_a0, od_a1, od_a2, od_g1, od_g2, od_k_k, od_k_a, od_r_k, od_ln_w, od_ln_b, od_w_o):
    b, t, d = x.shape
    m = mem.shape[1]
    depth = norm_mix.shape[0]
    tm = min(512, t)
    for layer in range(depth):
        if layer % 2 == 0:
            e = layer // 2
            x = _even_mixer_layer(x, norm_mix[layer], ev_w_in[e], ev_conv_qk[e], ev_m_ig_bias[e], ev_m_fg_bias[e],
                                  ev_m_norm[e], ev_g_decay_w2[e], ev_g_decay_b[e], ev_g_norm[e], ev_w_out[e], tm=tm)
        else:
            o = layer // 2
            x = _rwkv_mixer_layer(x, norm_mix[layer], od_mu[o], od_w_rkv[o], od_w0[o], od_w1[o], od_w2[o], od_a0[o],
                                  od_a1[o], od_a2[o], od_g1[o], od_g2[o], od_k_k[o], od_k_a[o], od_r_k[o],
                                  od_ln_w[o], od_ln_b[o], od_w_o[o], tm=tm)
        kv = rms_matmul(mem.reshape(b * m, d), norm_mem[layer], _mx(xa_wkv[layer]), tm=min(512, b * m),
                        out_dtype=MXU_DTYPE).reshape(b, m, 2 * d)
        x = cross_attention(x, kv, norm_xattn[layer], xa_wq[layer], xa_wo[layer], tm=tm)
        last = layer == depth - 1
        x = swiglu_ffn(x.reshape(b * t, d), norm_ffn[layer], ffn_w_gate[layer], ffn_w_up[layer], ffn_w_down[layer],
                       norm_final, tm=tm, final_norm=last).reshape(b, t, d)
    return x
```

```python
import functools
import math

import jax
import jax.numpy as jnp
from jax import lax
from jax.experimental import pallas as pl
from jax.experimental.pallas import tpu as pltpu

F32 = jnp.float32
MXU_DTYPE = jnp.bfloat16
LANES = 128
SUBLANES = 8
VMEM_LIMIT_BYTES = 60 * 1024 * 1024
NEG_BIG = -1e30

EPS = 1e-6
CONV_WIDTH = 3
M_HEADS = 4
M_HEAD_DIM = 128
G_HEADS = 4
G_KEY_DIM = 64
G_VAL_DIM = 128
G_DECAY_RANK = 16
G_DECAY_TAU = 16.0
R_HEAD_DIM = 64
R_GN_EPS = 64e-5
X_HEADS = 4

MLSTM_CHUNK = 256
GLA_CHUNK = 64
RWKV_CHUNK = 64
MLSTM_GROUP_CHUNKS = 2
GLA_GROUP_CHUNKS = 4
RWKV_BUILD_CHUNKS = 4

_EV_MQ, _EV_MK, _EV_MV, _EV_MO = 0, 4, 8, 12
_EV_GQ, _EV_GK, _EV_GV, _EV_GG, _EV_GATE = 16, 18, 20, 24, 28
_EV_BLOCKS = 29
_GATE_IG, _GATE_FG, _GATE_LR = 0, 8, 16


def _mx(a):
    return a.astype(MXU_DTYPE)


def _dot(a, b):
    return jnp.dot(_mx(a), _mx(b), preferred_element_type=F32)


def _dot_nt(a, b):
    return lax.dot_general(_mx(a), _mx(b), (((1,), (1,)), ((), ())), preferred_element_type=F32)


def _dot_tn(a, b):
    return lax.dot_general(_mx(a), _mx(b), (((0,), (0,)), ((), ())), preferred_element_type=F32)


def _scan_cumsum(mask, x, one_matmul=True):
    m = _mx(mask)
    hi = _mx(x)
    rest = x - hi.astype(F32)
    mid = _mx(rest)
    lo = _mx(rest - mid.astype(F32))
    if one_matmul:
        width = x.shape[1]
        sums = jnp.dot(m, jnp.concatenate([hi, mid, lo], axis=1), preferred_element_type=F32)
        return sums[:, :width] + sums[:, width:2 * width] + sums[:, 2 * width:]
    dot = lambda t: jnp.dot(m, t, preferred_element_type=F32)
    return dot(hi) + dot(mid) + dot(lo)


def _rms(x, g):
    return x * lax.rsqrt(jnp.mean(x * x, axis=-1, keepdims=True) + EPS) * g


def _log_sigmoid(x):
    return jnp.minimum(x, 0.0) - jnp.log1p(jnp.exp(-jnp.abs(x)))


def _silu(x):
    return x * jax.nn.sigmoid(x)


def _params(*sem):
    return pltpu.CompilerParams(dimension_semantics=sem, vmem_limit_bytes=VMEM_LIMIT_BYTES)


def _resident(shape):
    nd = len(shape)
    return pl.BlockSpec(shape, lambda *_: (0,) * nd, pipeline_mode=pl.Buffered(1))


def _scan_masks(n, reverse):
    row = lax.broadcasted_iota(jnp.int32, (n, n), 0)
    col = lax.broadcasted_iota(jnp.int32, (n, n), 1)
    if reverse:
        return col >= row, col > row
    return col <= row, col < row


def _rms_matmul_kernel(x_ref, g_ref, w_ref, o_ref):
    h = _rms(x_ref[...], g_ref[...])
    o_ref[...] = _dot(h, w_ref[...]).astype(o_ref.dtype)


def rms_matmul(x2d, g, w, *, tm, out_dtype):
    n, d = x2d.shape
    nout = w.shape[1]
    return pl.pallas_call(
        _rms_matmul_kernel,
        grid=(n // tm,),
        in_specs=[pl.BlockSpec((tm, d), lambda i: (i, 0)), _resident((1, d)), _resident((d, nout))],
        out_specs=pl.BlockSpec((tm, nout), lambda i: (i, 0)),
        out_shape=jax.ShapeDtypeStruct((n, nout), out_dtype),
        compiler_params=_params("parallel"),
        name="rms_matmul",
    )(x2d, g.reshape(1, d), w)


def _extract_col(x, lane_idx):
    lane = lax.broadcasted_iota(jnp.int32, x.shape, 1)
    return jnp.sum(jnp.where(lane == lane_idx, x, 0.0), axis=1, keepdims=True)


def _mlstm_kernel(q_ref, k_ref, v_ref, o_ref, gate_ref, conv_ref, gbias_ref, norm_ref, out_ref,
                  qc_ref, kc_ref, gl_ref, hf_ref, hb_ref, c_ref, m_ref, *, seq, chunk):
    L = chunk
    nc = seq // L
    d = M_HEAD_DIM
    head = pl.program_id(1)

    row_id = lax.broadcasted_iota(jnp.int32, (L, d), 0)
    lane_id = lax.broadcasted_iota(jnp.int32, (L, LANES), 1)
    is_fg = (lane_id >= _GATE_FG) & (lane_id < _GATE_LR)

    def conv_silu(src_ref, w, r0, c):
        x = src_ref[0, pl.ds(r0, L), :]
        prev_row = src_ref[0, pl.ds(jnp.maximum(r0 - 1, 0), 1), :] * jnp.where(c > 0, 1.0, 0.0)
        next_row = src_ref[0, pl.ds(jnp.minimum(r0 + L, seq - 1), 1), :] * jnp.where(c < nc - 1, 1.0, 0.0)
        x_prev = jnp.where(row_id == 0, prev_row, pltpu.roll(x, 1, axis=0))
        x_next = jnp.where(row_id == L - 1, next_row, pltpu.roll(x, L - 1, axis=0))
        return _silu(x_prev * w[0:1, :] + x * w[1:2, :] + x_next * w[2:3, :])

    def prologue(c, carry):
        r0 = pl.multiple_of(c * L, L)
        qc_ref[pl.ds(r0, L), :] = (conv_silu(q_ref, conv_ref[0], r0, c) * (d ** -0.5)).astype(qc_ref.dtype)
        kc_ref[pl.ds(r0, L), :] = conv_silu(k_ref, conv_ref[1], r0, c)
        g = gate_ref[0, pl.ds(r0, L), :] + gbias_ref[...]
        gl_ref[pl.ds(r0, L), :] = jnp.where(is_fg, _log_sigmoid(g), g)
        return carry

    lax.fori_loop(0, nc, prologue, 0)

    c_ref[...] = jnp.zeros_like(c_ref)
    m_ref[...] = jnp.zeros_like(m_ref)
    ones = jnp.ones((L, d), MXU_DTYPE)

    masks = [_scan_masks(L, reverse)[0] for reverse in (False, True)]
    dst_refs = (hf_ref, hb_ref)
    cb = MLSTM_GROUP_CHUNKS
    lane_sel = lax.broadcasted_iota(jnp.int32, (L, LANES), 1)

    def split_lanes(col, first):
        hi = _mx(col).astype(F32)
        mid = _mx(col - hi).astype(F32)
        lo = col - hi - mid
        out = jnp.where(lane_sel < 6, 1.0, 0.0)
        for offset, term in enumerate((hi, mid, lo)):
            out = jnp.where(lane_sel == first + offset, term, out)
        return _mx(out)

    def body(i, carry):
        chains = [(dr, j) for j in range(cb) for dr in range(2)]
        chunk_of = lambda dr, j: (i * cb + j) if dr == 0 else (nc - 1 - (i * cb + j))
        rows = [pl.ds(pl.multiple_of(chunk_of(dr, j) * L, L), L) for dr, j in chains]
        each = lambda f, *lists: [f(*args) for args in zip(*lists)]

        g = each(lambda r_: gl_ref[r_, :], rows)
        bcum = each(lambda dj, g_: _scan_cumsum(masks[dj[0]], g_, one_matmul=False), chains, g)
        ig_col = each(lambda dj, g_: _extract_col(g_, _GATE_IG + dj[0] * M_HEADS + head), chains, g)
        b_col = each(lambda dj, b_: _extract_col(b_, _GATE_FG + dj[0] * M_HEADS + head), chains, bcum)
        g_tot = each(lambda dj, b_: b_[0:1, :] if dj[0] == 1 else b_[L - 1:L, :], chains, b_col)
        dmat = each(lambda dj, b_, i_: jnp.where(masks[dj[0]], _dot_nt(split_lanes(b_, 0), split_lanes(i_ - b_, 3)),
                                                 NEG_BIG), chains, b_col, ig_col)
        dmax = each(lambda d_: jnp.max(d_, axis=1, keepdims=True), dmat)
        q = each(lambda r_: qc_ref[r_, :], rows)
        k = each(lambda r_: kc_ref[r_, :], rows)
        v_aug = each(lambda r_: jnp.concatenate([_mx(v_ref[0, r_, :]), ones], axis=1), rows)
        sc = each(lambda q_, k_, d_, m_: _mx(_dot_nt(q_, k_) * jnp.exp(d_ - m_)), q, k, dmat, dmax)
        nd_intra = each(_dot, sc, v_aug)
        kw_log = each(lambda g_, b_, i_: g_ - b_ + i_, g_tot, b_col, ig_col)
        kw_max = each(lambda w_: jnp.max(w_, axis=0, keepdims=True), kw_log)
        d_c = each(lambda k_, w_, a_, v_: _dot_tn(_mx(k_ * jnp.exp(w_ - a_)), v_), k, kw_log, kw_max, v_aug)

        c_start = [None] * len(chains)
        m_start = [None] * len(chains)
        for dr in range(2):
            c_cur = c_ref[dr]
            m_cur = m_ref[dr][:, 0:1]
            for j in range(cb):
                idx = chains.index((dr, j))
                c_start[idx], m_start[idx] = _mx(c_cur), m_cur
                m_new = jnp.maximum(g_tot[idx] + m_cur, kw_max[idx])
                c_cur = jnp.exp(g_tot[idx] + m_cur - m_new) * c_cur + jnp.exp(kw_max[idx] - m_new) * d_c[idx]
                m_cur = m_new
            c_ref[dr] = c_cur
            m_ref[dr] = jnp.broadcast_to(m_cur, (1, LANES))

        nd_inter = each(_dot, q, c_start)
        for idx, (dr, j) in enumerate(chains):
            inter_log = b_col[idx] + m_start[idx]
            m_t = jnp.maximum(inter_log, dmax[idx])
            numden = jnp.exp(dmax[idx] - m_t) * nd_intra[idx] + jnp.exp(inter_log - m_t) * nd_inter[idx]
            dst_refs[dr][rows[idx], :] = numden[:, :d] / jnp.maximum(jnp.abs(numden[:, d:]), jnp.exp(-m_t))
        return carry

    lax.fori_loop(0, nc // cb, body, 0)

    def epilogue(c, carry):
        r0 = pl.multiple_of(c * L, L)
        h = hf_ref[pl.ds(r0, L), :] + hb_ref[pl.ds(r0, L), :]
        h = _rms(h, norm_ref[...])
        out_ref[0, pl.ds(r0, L), :] = h * jax.nn.sigmoid(o_ref[0, pl.ds(r0, L), :])
        return carry

    lax.fori_loop(0, nc, epilogue, 0)


def mlstm_mixer(p, conv_qk, gate_bias, m_norm, *, chunk):
    b, t, _ = p.shape
    d = M_HEAD_DIM
    blk = lambda off: pl.BlockSpec((1, t, d), lambda i, h, off=off: (i, 0, off + h))
    conv = conv_qk.reshape(CONV_WIDTH, 2, M_HEADS, d).transpose(2, 1, 0, 3)
    return pl.pallas_call(
        functools.partial(_mlstm_kernel, seq=t, chunk=chunk),
        grid=(b, M_HEADS),
        in_specs=[blk(_EV_MQ), blk(_EV_MK), blk(_EV_MV), blk(_EV_MO),
                  pl.BlockSpec((1, t, LANES), lambda i, h: (i, 0, _EV_GATE)),
                  pl.BlockSpec((None, 2, CONV_WIDTH, d), lambda i, h: (h, 0, 0, 0)),
                  pl.BlockSpec((1, LANES), lambda i, h: (0, 0)),
                  pl.BlockSpec((1, d), lambda i, h: (0, h))],
        out_specs=pl.BlockSpec((1, t, d), lambda i, h: (i, 0, h)),
        out_shape=jax.ShapeDtypeStruct((b, t, M_HEADS * d), F32),
        scratch_shapes=[pltpu.VMEM((t, d), MXU_DTYPE),
                        pltpu.VMEM((t, d), F32),
                        pltpu.VMEM((t, LANES), F32),
                        pltpu.VMEM((t, d), F32),
                        pltpu.VMEM((t, d), F32),
                        pltpu.VMEM((2, d, 2 * d), F32),
                        pltpu.VMEM((2, 1, LANES), F32)],
        compiler_params=_params("parallel", "parallel"),
        name="mlstm",
    )(p, p, p, p, p, conv, gate_bias, m_norm.reshape(1, M_HEADS * d))


def _gla_kernel(q_ref, k_ref, v0_ref, v1_ref, g0_ref, g1_ref, gate_ref, w2_ref, b2_ref, norm_ref, out_ref,
                of_ref, ob_ref, s_ref, *, seq, chunk):
    L = chunk
    nc = seq // L
    dk2 = 2 * G_KEY_DIM
    dv = G_VAL_DIM
    lane_k = lax.broadcasted_iota(jnp.int32, (L, dk2), 1)
    head0 = (lane_k < G_KEY_DIM).astype(F32)
    head1 = 1.0 - head0
    s_row = lax.broadcasted_iota(jnp.int32, (2 * dv, dk2), 0)
    s_col = lax.broadcasted_iota(jnp.int32, (2 * dv, dk2), 1)
    s_mask = ((s_row < dv) == (s_col < G_KEY_DIM)).astype(F32)

    s_ref[...] = jnp.zeros_like(s_ref)
    masks = [_scan_masks(L, reverse)[0] for reverse in (False, True)]
    masks2 = [jnp.concatenate([m, m], axis=0) for m in masks]
    dst_refs = (of_ref, ob_ref)
    cb = GLA_GROUP_CHUNKS

    def body(i, carry):
        chains = [(d, j) for j in range(cb) for d in range(2)]
        chunk_of = lambda d, j: (i * cb + j) if d == 0 else (nc - 1 - (i * cb + j))
        rows = [pl.ds(pl.multiple_of(chunk_of(d, j) * L, L), L) for d, j in chains]
        each = lambda f, *lists: [f(*args) for args in zip(*lists)]

        z = each(lambda dj, r_: _dot(gate_ref[0, r_, :], w2_ref[dj[0], 0]) + b2_ref[dj[0], 0], chains, rows)
        la = each(lambda z_: _log_sigmoid(z_) * (1.0 / G_DECAY_TAU), z)
        bcum = each(lambda dj, l_: _scan_cumsum(masks[dj[0]], l_), chains, la)
        g_tot = each(lambda dj, b_: b_[0:1, :] if dj[0] == 1 else b_[L - 1:L, :], chains, bcum)
        mid = each(lambda b_: b_[L // 2:L // 2 + 1, :], bcum)
        q = each(lambda r_: q_ref[0, r_, :] * (G_KEY_DIM ** -0.5), rows)
        k = each(lambda r_: k_ref[0, r_, :], rows)
        v = each(lambda r_: jnp.concatenate([_mx(v0_ref[0, r_, :]), _mx(v1_ref[0, r_, :])], axis=1), rows)
        q_in = each(lambda q_, b_, m_: q_ * jnp.exp(b_ - m_), q, bcum, mid)
        q_in2 = each(lambda q_: _mx(jnp.concatenate([q_ * head0, q_ * head1], axis=0)), q_in)
        k_in = each(lambda k_, b_, m_: _mx(k_ * jnp.exp(m_ - b_)), k, bcum, mid)
        scores = each(lambda dj, q_, k_: _mx(jnp.where(masks2[dj[0]], _dot_nt(q_, k_), 0.0)), chains, q_in2, k_in)
        o_intra = each(lambda a_, v_: jnp.concatenate([_dot(a_[:L], v_[:, :dv]), _dot(a_[L:], v_[:, dv:])], axis=1),
                       scores, v)
        q_hat = each(lambda q_, b_: _mx(q_ * jnp.exp(b_)), q, bcum)
        k_hat = each(lambda k_, b_, g_: _mx(k_ * jnp.exp(g_ - b_)), k, bcum, g_tot)
        d_s = each(lambda v_, k_: _dot_tn(v_, k_) * s_mask, v, k_hat)
        decay = each(jnp.exp, g_tot)

        s_at_start = [None] * len(chains)
        for d in range(2):
            s_cur = s_ref[d]
            for j in range(cb):
                idx = chains.index((d, j))
                s_at_start[idx] = _mx(s_cur)
                s_cur = s_cur * decay[idx] + d_s[idx]
            s_ref[d] = s_cur
        o_inter = each(_dot_nt, q_hat, s_at_start)
        for idx, (d, j) in enumerate(chains):
            dst_refs[d][rows[idx], :] = o_intra[idx] + o_inter[idx]
        return carry

    lax.fori_loop(0, nc // cb, body, 0)

    def epilogue(c, carry):
        r0 = pl.multiple_of(c * L, L)
        o = of_ref[pl.ds(r0, L), :] + ob_ref[pl.ds(r0, L), :]
        g = jnp.concatenate([g0_ref[0, pl.ds(r0, L), :], g1_ref[0, pl.ds(r0, L), :]], axis=1)
        nrm = norm_ref[...]
        o = jnp.concatenate([_rms(o[:, :dv], nrm[:, :dv]), _rms(o[:, dv:], nrm[:, dv:])], axis=1)
        out_ref[0, pl.ds(r0, L), :] = o * _silu(g)
        return carry

    lax.fori_loop(0, nc, epilogue, 0)


def gla_mixer(p, decay_w2, decay_b, g_norm, *, chunk):
    b, t, _ = p.shape
    dv = G_VAL_DIM
    pairs = G_HEADS // 2
    dk2 = 2 * G_KEY_DIM
    w2 = jnp.zeros((2, LANES, pairs * dk2), F32)
    for direction in range(2):
        lo = _GATE_LR + direction * G_DECAY_RANK
        w2 = w2.at[direction, lo:lo + G_DECAY_RANK].set(decay_w2[direction])
    w2 = _mx(w2.reshape(2, LANES, pairs, dk2).transpose(0, 2, 1, 3))
    b2 = decay_b.reshape(2, pairs, 1, dk2)
    blk = lambda off, mult: pl.BlockSpec((1, t, LANES), lambda i, h, off=off, mult=mult: (i, 0, off + mult * h))
    return pl.pallas_call(
        functools.partial(_gla_kernel, seq=t, chunk=chunk),
        grid=(b, pairs),
        in_specs=[blk(_EV_GQ, 1), blk(_EV_GK, 1), blk(_EV_GV, 2), blk(_EV_GV + 1, 2),
                  blk(_EV_GG, 2), blk(_EV_GG + 1, 2),
                  pl.BlockSpec((1, t, LANES), lambda i, h: (i, 0, _EV_GATE)),
                  pl.BlockSpec((2, 1, LANES, dk2), lambda i, h: (0, h, 0, 0)),
                  pl.BlockSpec((2, 1, 1, dk2), lambda i, h: (0, h, 0, 0)),
                  pl.BlockSpec((1, 2 * dv), lambda i, h: (0, h))],
        out_specs=pl.BlockSpec((1, t, 2 * dv), lambda i, h: (i, 0, h)),
        out_shape=jax.ShapeDtypeStruct((b, t, G_HEADS * dv), F32),
        scratch_shapes=[pltpu.VMEM((t, 2 * dv), F32),
                        pltpu.VMEM((t, 2 * dv), F32),
                        pltpu.VMEM((2, 2 * dv, dk2), F32)],
        compiler_params=_params("parallel", "parallel"),
        name="gla",
    )(p, p, p, p, p, p, p, w2, b2, g_norm.reshape(1, G_HEADS * dv))


def _even_out_kernel(x_ref, hm_ref, og_ref, wm_ref, wg_ref, o_ref):
    o_ref[...] = x_ref[...] + _dot(hm_ref[...], wm_ref[...]) + _dot(og_ref[...], wg_ref[...])


def even_out_proj(x2d, hm, og, w_out, *, tm):
    n, d = x2d.shape
    wm, wg = _mx(w_out[:hm.shape[1]]), _mx(w_out[hm.shape[1]:])
    row = lambda width: pl.BlockSpec((tm, width), lambda i: (i, 0))
    return pl.pallas_call(
        _even_out_kernel,
        grid=(n // tm,),
        in_specs=[row(d), row(hm.shape[1]), row(og.shape[1]), _resident(wm.shape), _resident(wg.shape)],
        out_specs=row(d),
        out_shape=jax.ShapeDtypeStruct((n, d), F32),
        compiler_params=_params("parallel"),
        name="even_out_proj",
    )(x2d, hm, og, wm, wg)


def _gated_out_kernel(x_ref, y_ref, g_ref, w_ref, o_ref):
    o_ref[...] = x_ref[...] + _dot(y_ref[...] * g_ref[...], w_ref[...])


def gated_out_proj(x2d, y, g, w, *, tm):
    n, d = x2d.shape
    row = pl.BlockSpec((tm, d), lambda i: (i, 0))
    return pl.pallas_call(
        _gated_out_kernel,
        grid=(n // tm,),
        in_specs=[row, row, row, _resident(w.shape)],
        out_specs=row,
        out_shape=jax.ShapeDtypeStruct((n, d), F32),
        compiler_params=_params("parallel"),
        name="gated_out_proj",
    )(x2d, y, g, _mx(w))


def _rwkv_in_kernel(x_ref, xp_ref, xn_ref, nrm_ref, mu_ref, wrkv_ref, w1_ref, w2_ref, w0_ref,
                    a1_ref, a2_ref, a0_ref, g1_ref, g2_ref,
                    r_ref, k_ref, v_ref, a_ref, lwf_ref, lwb_ref, g_ref, *, tm):
    i = pl.program_id(1)
    nt = pl.num_programs(1)
    nrm = nrm_ref[...]
    h = _rms(x_ref[0], nrm)
    before = _rms(xp_ref[0, 0], nrm)[SUBLANES - 1:SUBLANES, :] * jnp.where(i > 0, 1.0, 0.0)
    after = _rms(xn_ref[0, 0], nrm)[0:1, :] * jnp.where(i < nt - 1, 1.0, 0.0)
    row = lax.broadcasted_iota(jnp.int32, h.shape, 0)
    h_prev = jnp.where(row == 0, before, pltpu.roll(h, 1, axis=0))
    h_next = jnp.where(row == tm - 1, after, pltpu.roll(h, tm - 1, axis=0))
    hh = 0.5 * (h_prev + h_next) - h
    mix = lambda j: h + hh * mu_ref[j:j + 1, :]
    xr, xw, xk, xv, xa, xg = (mix(j) for j in range(6))
    r_ref[0] = _dot(xr, wrkv_ref[0])
    k_ref[0] = _dot(xk, wrkv_ref[1])
    v_ref[0] = _dot(xv, wrkv_ref[2])
    a_ref[0] = jax.nn.sigmoid(a0_ref[...] + _dot(_dot(xa, a1_ref[...]), a2_ref[...]))
    g_ref[0] = _dot(jax.nn.sigmoid(_dot(xg, g1_ref[...])), g2_ref[...])
    for direction, dst in ((0, lwf_ref), (1, lwb_ref)):
        z = w0_ref[direction:direction + 1, :] + _dot(jnp.tanh(_dot(xw, w1_ref[direction])), w2_ref[direction])
        dst[0] = jax.nn.sigmoid(z) * -math.exp(-0.5)


def rwkv_in(x, norm, mu, w_rkv, w0, w1, w2, a0, a1, a2, g1, g2, *, tm):
    b, t, d = x.shape
    x8 = x.reshape(b, t // SUBLANES, SUBLANES, d)
    tb = tm // SUBLANES
    nb = t // SUBLANES
    rank_g = g1.shape[1]
    pad_g = -rank_g % LANES
    g1p = jnp.pad(g1, ((0, 0), (0, pad_g)))
    g2p = jnp.pad(g2, ((0, pad_g), (0, 0)))
    tile = pl.BlockSpec((1, tm, d), lambda i, j: (i, j, 0))
    out = jax.ShapeDtypeStruct((b, t, d), F32)
    weights = [norm.reshape(1, d), mu, _mx(w_rkv), _mx(w1), _mx(w2), w0,
               _mx(a1), _mx(a2), a0.reshape(1, d), _mx(g1p), _mx(g2p)]
    return pl.pallas_call(
        functools.partial(_rwkv_in_kernel, tm=tm),
        grid=(b, t // tm),
        in_specs=[tile,
                  pl.BlockSpec((1, 1, SUBLANES, d), lambda i, j: (i, jnp.maximum(j * tb - 1, 0), 0, 0)),
                  pl.BlockSpec((1, 1, SUBLANES, d), lambda i, j: (i, jnp.minimum((j + 1) * tb, nb - 1), 0, 0))]
                 + [_resident(w.shape) for w in weights],
        out_specs=[tile] * 7,
        out_shape=[out] * 7,
        compiler_params=_params("parallel", "parallel"),
        name="rwkv_in",
    )(x, x8, x8, *weights)


def _rwkv_scan_kernel(r_ref, k_ref, v_ref, a_ref, lwf_ref, lwb_ref, kk_ref, ka_ref, rk_ref, lnw_ref, lnb_ref,
                      out_ref, y_ref, rq_ref, m_ref, sadd_ref, gam_ref, s_ref, *, seq, chunk):
    L = chunk
    nc = seq // L
    n = R_HEAD_DIM
    lane = lax.broadcasted_iota(jnp.int32, (L, 2 * n), 1)
    head0 = (lane < n).astype(F32)
    head1 = 1.0 - head0

    def head_sum(x):
        s0 = jnp.sum(x * head0, axis=1, keepdims=True)
        s1 = jnp.sum(x * head1, axis=1, keepdims=True)
        return s0 * head0 + s1 * head1

    head0_mx, head1_mx = _mx(head0), _mx(head1)

    def stack(x):
        x = _mx(x)
        return jnp.concatenate([x * head0_mx, x * head1_mx], axis=0)

    def load(c):
        r0 = pl.multiple_of(c * L, L)
        rows = pl.ds(r0, L)
        r, k, v, a = r_ref[0, rows, :], k_ref[0, rows, :], v_ref[0, rows, :], a_ref[0, rows, :]
        kk = k * kk_ref[...]
        kk = kk / jnp.maximum(jnp.sqrt(head_sum(kk * kk)), 1e-12)
        kp = k * (1.0 + (a - 1.0) * ka_ref[...])
        return rows, r, kp, v, -kk, kk * a

    lane2 = lax.broadcasted_iota(jnp.int32, (L, 4 * n), 1)
    head0_2 = ((lane2 & (2 * n - 1)) < n).astype(F32)
    head0_2_mx, head1_2_mx = _mx(head0_2), _mx(1.0 - head0_2)

    def stack2(x):
        x = _mx(x)
        return jnp.concatenate([x * head0_2_mx, x * head1_2_mx], axis=0)

    masks_l = [_scan_masks(L, reverse)[0] for reverse in (False, True)]
    row_cat = lax.broadcasted_iota(jnp.int32, (L, 2 * L), 0)
    col_cat = lax.broadcasted_iota(jnp.int32, (L, 2 * L), 1) & (L - 1)
    masks = [(col_cat <= row_cat, col_cat < row_cat), (col_cat >= row_cat, col_cat > row_cat)]
    eye_cat = (col_cat == row_cat).astype(F32)
    same_head = ((lax.broadcasted_iota(jnp.int32, (2 * n, 2 * n), 0) < n)
                 == (lax.broadcasted_iota(jnp.int32, (2 * n, 2 * n), 1) < n)).astype(F32)
    lw_refs = (lwf_ref, lwb_ref)
    cb = min(RWKV_BUILD_CHUNKS, nc // 2)
    groups = nc // cb
    chunk_of = lambda d, g, j: (g * cb + j) if d == 0 else (nc - 1 - (g * cb + j))
    each = lambda f, *lists: [f(*args) for args in zip(*lists)]

    def finish(c, y):
        rows = pl.ds(pl.multiple_of(c * L, L), L)
        r, k, v, a = r_ref[0, rows, :], k_ref[0, rows, :], v_ref[0, rows, :], a_ref[0, rows, :]
        kp = k * (1.0 + (a - 1.0) * ka_ref[...])
        mean = head_sum(y) * (1.0 / n)
        yc = y - mean
        var = head_sum(yc * yc) * (1.0 / n)
        yn = yc * lax.rsqrt(var + R_GN_EPS) * lnw_ref[...] + lnb_ref[...]
        out_ref[0, rows, :] = yn + head_sum(r * kp * rk_ref[...]) * v

    def sweep_steps(g, crossed):
        state = [s_ref[d] for d in range(2)]

        def step(j):
            cs = [chunk_of(d, g, j) for d in range(2)]
            rows = [pl.ds(pl.multiple_of(c * L, L), L) for c in cs]
            s_mx = each(_mx, state)
            y = [y_ref[d, rows[d], :] + _dot_nt(rq_ref[d, rows[d], :], s_mx[d]) for d in range(2)]
            for d in range(2):
                state[d] = (state[d] * gam_ref[d, pl.ds(cs[d], 1), :] + _dot(s_mx[d], m_ref[d, cs[d]])
                            + sadd_ref[d, cs[d]])
            if j == cb - 1:
                for d in range(2):
                    s_ref[d] = state[d]
            for d in range(2):
                if crossed:
                    finish(cs[d], y[d] + y_ref[1 - d, rows[d], :])
                else:
                    y_ref[d, rows[d], :] = y[d]

        return [functools.partial(step, j) for j in range(cb)]

    def build_group(g, interleaved=()):
        pending = list(interleaved)
        run_one = lambda: pending.pop(0)() if pending else None
        chains = [(d, j) for j in range(cb) for d in range(2)]
        chunk_ids = [chunk_of(d, g, j) for d, j in chains]
        loaded = [load(c) for c in chunk_ids]

        lw = each(lambda dj, ld: lw_refs[dj[0]][0, ld[0], :], chains, loaded)
        cin = each(lambda dj, x: _scan_cumsum(masks_l[dj[0]], x), chains, lw)
        ctot = each(lambda dj, c_: c_[0:1, :] if dj[0] == 1 else c_[L - 1:L, :], chains, cin)
        e_out = each(lambda c_: jnp.exp(-c_), cin)
        e_end = each(lambda c_, t_: jnp.exp(t_ - c_), cin, ctot)
        a_hat = each(lambda ld, c_, l_: ld[4] * jnp.exp(c_ - l_), loaded, cin, lw)
        r_hat = each(lambda ld, c_: ld[1] * jnp.exp(c_), loaded, cin)
        x = each(lambda a_, r_: _mx(jnp.concatenate([a_, r_], axis=0)), a_hat, r_hat)
        yk = each(lambda ld, e_: jnp.concatenate([stack(ld[5] * e_), stack(ld[2] * e_)], axis=0), loaded, e_out)
        run_one()
        gram = each(_dot_nt, x, yk)
        a_ab = each(lambda dj, g_: jnp.where(masks[dj[0]][1], g_[:L, :2 * L], 0.0), chains, gram)
        a_ak = each(lambda dj, g_: _mx(jnp.where(masks[dj[0]][1], g_[:L, 2 * L:], 0.0)), chains, gram)
        a_rb = each(lambda dj, g_: _mx(jnp.where(masks[dj[0]][0], g_[L:, :2 * L], 0.0)), chains, gram)
        a_rk = each(lambda dj, g_: _mx(jnp.where(masks[dj[0]][0], g_[L:, 2 * L:], 0.0)), chains, gram)

        t_inv = each(lambda n_: eye_cat + n_, a_ab)
        power = each(lambda n_: _mx(_dot(n_, stack(n_))), a_ab)
        for _ in range(L.bit_length() - 3):
            both = each(lambda p_, t_: _dot(p_, stack2(jnp.concatenate([p_, _mx(t_)], axis=1))), power, t_inv)
            power = each(lambda b_: _mx(b_[:, :2 * L]), both)
            t_inv = each(lambda t_, b_: t_ + b_[:, 2 * L:], t_inv, both)
            run_one()
        t_inv = each(lambda p_, t_: t_ + _dot(p_, stack(t_)), power, t_inv)

        v2 = each(lambda ld: stack(ld[3]), loaded)
        av = each(_dot, a_ak, v2)
        w = each(lambda t_, a_, av_: _dot(t_, stack2(jnp.concatenate([_mx(a_), _mx(av_)], axis=1))),
                 t_inv, a_hat, av)
        z = each(lambda b_, w_: _dot(b_, stack2(w_)), a_rb, w)
        y0 = each(lambda z_, k_, v_: z_[:, 2 * n:] + _dot(k_, v_), z, a_rk, v2)
        while pending:
            run_one()
        b_end = each(lambda ld, e_: _mx(ld[5] * e_), loaded, e_end)
        k_end = each(lambda ld, e_: _mx(ld[2] * e_), loaded, e_end)
        m_lr = each(lambda w_, b_: _dot_tn(w_[:, :2 * n], b_) * same_head, w, b_end)
        s_add = each(lambda w_, ld, b_, k_: _dot_tn(jnp.concatenate([w_[:, 2 * n:], ld[3]], axis=0),
                                                    jnp.concatenate([b_, k_], axis=0)) * same_head,
                     w, loaded, b_end, k_end)
        for idx, (d, j) in enumerate(chains):
            rows = loaded[idx][0]
            y_ref[d, rows, :] = y0[idx]
            rq_ref[d, rows, :] = (r_hat[idx] + z[idx][:, :2 * n]).astype(rq_ref.dtype)
            m_ref[d, chunk_ids[idx]] = m_lr[idx].astype(m_ref.dtype)
            sadd_ref[d, chunk_ids[idx]] = s_add[idx]
            gam_ref[d, pl.ds(chunk_ids[idx], 1), :] = jnp.exp(ctot[idx])

    s_ref[...] = jnp.zeros_like(s_ref)
    build_group(0)

    def before_crossing(g, carry):
        build_group(g, sweep_steps(g - 1, crossed=False))
        return carry

    def after_crossing(g, carry):
        build_group(g, sweep_steps(g - 1, crossed=True))
        return carry

    lax.fori_loop(1, groups // 2 + 1, before_crossing, 0)
    lax.fori_loop(groups // 2 + 1, groups, after_crossing, 0)
    for step in sweep_steps(groups - 1, crossed=groups > 1):
        step()


def rwkv_scan(r, k, v, a, lwf, lwb, k_k, k_a, r_k, ln_w, ln_b, *, chunk):
    b, t, d = r.shape
    w = 2 * R_HEAD_DIM
    pairs = d // w
    nc = t // chunk
    assert chunk == R_HEAD_DIM and nc % 2 == 0, (chunk, t)
    blk = pl.BlockSpec((1, t, w), lambda i, h: (i, 0, h))
    vec = pl.BlockSpec((1, w), lambda i, h: (0, h))
    row = lambda z: z.reshape(1, d)
    return pl.pallas_call(
        functools.partial(_rwkv_scan_kernel, seq=t, chunk=chunk),
        grid=(b, pairs),
        in_specs=[blk] * 6 + [vec] * 5,
        out_specs=blk,
        out_shape=jax.ShapeDtypeStruct((b, t, d), F32),
        scratch_shapes=[pltpu.VMEM((2, t, w), F32),
                        pltpu.VMEM((2, t, w), MXU_DTYPE),
                        pltpu.VMEM((2, nc, w, w), MXU_DTYPE),
                        pltpu.VMEM((2, nc, w, w), F32),
                        pltpu.VMEM((2, nc, w), F32),
                        pltpu.VMEM((2, w, w), F32)],
        compiler_params=_params("parallel", "parallel"),
        name="rwkv_scan",
    )(r, k, v, a, lwf, lwb, row(k_k), row(k_a), row(r_k), row(ln_w), row(ln_b))


def _xattn_kernel(x_ref, nrm_ref, wq_ref, k_ref, v_ref, wo_ref, o_ref):
    x = x_ref[0]
    d = x.shape[-1]
    dh = d // X_HEADS
    q = _mx(_dot(_rms(x, nrm_ref[...]), wq_ref[...]))
    cols = [slice(hd * dh, (hd + 1) * dh) for hd in range(X_HEADS)]
    s = [_dot_nt(q[:, c], k_ref[0, :, c]) * (dh ** -0.5) for c in cols]
    p = [jnp.exp(s_ - jnp.max(s_, axis=-1, keepdims=True)) for s_ in s]
    p = [_mx(p_ / jnp.sum(p_, axis=-1, keepdims=True)) for p_ in p]
    heads = [_mx(_dot(p_, v_ref[0, :, c])) for p_, c in zip(p, cols)]
    o_ref[0] = x + _dot(jnp.concatenate(heads, axis=1), wo_ref[...])


def cross_attention(x, kv, norm, wq, wo, *, tm):
    b, t, d = x.shape
    m = kv.shape[1]
    tile = pl.BlockSpec((1, tm, d), lambda i, j: (i, j, 0))
    return pl.pallas_call(
        _xattn_kernel,
        grid=(b, t // tm),
        in_specs=[tile, _resident((1, d)), _resident((d, d)),
                  pl.BlockSpec((1, m, d), lambda i, j: (i, 0, 0)),
                  pl.BlockSpec((1, m, d), lambda i, j: (i, 0, 1)),
                  _resident((d, d))],
        out_specs=tile,
        out_shape=jax.ShapeDtypeStruct((b, t, d), F32),
        compiler_params=_params("parallel", "parallel"),
        name="cross_attention",
    )(x, norm.reshape(1, d), _mx(wq), kv, kv, _mx(wo))


def _ffn_kernel(x_ref, nrm_ref, wg_ref, wu_ref, wd_ref, fin_ref, o_ref, *, final_norm):
    x = x_ref[...]
    h = _mx(_rms(x, nrm_ref[...]))
    act = _silu(_dot(h, wg_ref[...])) * _dot(h, wu_ref[...])
    y = x + _dot(act, wd_ref[...])
    o_ref[...] = _rms(y, fin_ref[...]) if final_norm else y


def swiglu_ffn(x2d, norm, w_gate, w_up, w_down, final_gain, *, tm, final_norm):
    n, d = x2d.shape
    dff = w_gate.shape[1]
    row = pl.BlockSpec((tm, d), lambda i: (i, 0))
    return pl.pallas_call(
        functools.partial(_ffn_kernel, final_norm=final_norm),
        grid=(n // tm,),
        in_specs=[row, _resident((1, d)), _resident((d, dff)), _resident((d, dff)), _resident((dff, d)),
                  _resident((1, d))],
        out_specs=row,
        out_shape=jax.ShapeDtypeStruct((n, d), F32),
        compiler_params=_params("parallel"),
        name="swiglu_ffn",
    )(x2d, norm.reshape(1, d), _mx(w_gate), _mx(w_up), _mx(w_down), final_gain.reshape(1, d))


def _pack_even_in(w_in):
    d = w_in.shape[0]
    mw, gkw, gvw = M_HEADS * M_HEAD_DIM, G_HEADS * G_KEY_DIM, G_HEADS * G_VAL_DIM
    widths = (mw, mw, mw, mw, 2 * M_HEADS, 2 * M_HEADS, gkw, gkw, gvw, gvw, 2 * G_DECAY_RANK)
    starts = [0]
    for wdt in widths:
        starts.append(starts[-1] + wdt)
    mq, mk, mv, mo, mi, mf, gq, gk, gv, gg, glr = (w_in[:, starts[j]:starts[j + 1]] for j in range(len(widths)))
    gate = jnp.concatenate([mi, mf, glr], axis=1)
    gate = jnp.pad(gate, ((0, 0), (0, LANES - gate.shape[1])))
    packed = jnp.concatenate([mq, mk, mv, mo, gq, gk, gv, gg, gate], axis=1)
    assert packed.shape == (d, _EV_BLOCKS * LANES)
    return _mx(packed)


def _even_mixer_layer(x, norm, w_in, conv_qk, ig_bias, fg_bias, m_norm, decay_w2, decay_b, g_norm, w_out, *, tm):
    b, t, d = x.shape
    x2d = x.reshape(b * t, d)
    p = rms_matmul(x2d, norm, _pack_even_in(w_in), tm=tm, out_dtype=F32).reshape(b, t, -1)
    gate_bias = jnp.concatenate([ig_bias.reshape(-1), fg_bias.reshape(-1)])
    gate_bias = jnp.pad(gate_bias, (0, LANES - gate_bias.shape[0])).reshape(1, LANES)
    hm = mlstm_mixer(p, conv_qk, gate_bias, m_norm, chunk=min(MLSTM_CHUNK, t))
    og = gla_mixer(p, decay_w2, decay_b, g_norm, chunk=min(GLA_CHUNK, t))
    x2d = even_out_proj(x2d, hm.reshape(b * t, -1), og.reshape(b * t, -1), w_out, tm=tm)
    return x2d.reshape(b, t, d)


def _rwkv_mixer_layer(x, norm, mu, w_rkv, w0, w1, w2, a0, a1, a2, g1, g2, k_k, k_a, r_k, ln_w, ln_b, w_o, *, tm):
    b, t, d = x.shape
    r, k, v, a, lwf, lwb, g = rwkv_in(x, norm, mu, w_rkv, w0, w1, w2, a0, a1, a2, g1, g2, tm=tm)
    y = rwkv_scan(r, k, v, a, lwf, lwb, k_k, k_a, r_k.reshape(-1), ln_w, ln_b, chunk=min(RWKV_CHUNK, t))
    x2d = gated_out_proj(x.reshape(b * t, d), y.reshape(b * t, d), g.reshape(b * t, d), w_o, tm=tm)
    return x2d.reshape(b, t, d)


def kernel(x, mem, norm_mix, norm_xattn, norm_mem, norm_ffn, norm_final, xa_wq, xa_wkv, xa_wo, ffn_w_gate, ffn_w_up, ffn_w_down, ev_w_in, ev_conv_qk, ev_m_ig_bias, ev_m_fg_bias, ev_m_norm, ev_g_decay_w2, ev_g_decay_b, ev_g_norm, ev_w_out, od_mu, od_w_rkv, od_w0, od_w1, od_w2, od_a0, od_a1, od_a2, od_g1, od_g2, od_k_k, od_k_a, od_r_k, od_ln_w, od_ln_b, od_w_o):
    b, t, d = x.shape
    m = mem.shape[1]
    depth = norm_mix.shape[0]
    tm = min(512, t)
    for layer in range(depth):
        if layer % 2 == 0:
            e = layer // 2
            x = _even_mixer_layer(x, norm_mix[layer], ev_w_in[e], ev_conv_qk[e], ev_m_ig_bias[e], ev_m_fg_bias[e],
                                  ev_m_norm[e], ev_g_decay_w2[e], ev_g_decay_b[e], ev_g_norm[e], ev_w_out[e], tm=tm)
        else:
            o = layer // 2
            x = _rwkv_mixer_layer(x, norm_mix[layer], od_mu[o], od_w_rkv[o], od_w0[o], od_w1[o], od_w2[o], od_a0[o],
                                  od_a1[o], od_a2[o], od_g1[o], od_g2[o], od_k_k[o], od_k_a[o], od_r_k[o],
                                  od_ln_w[o], od_ln_b[o], od_w_o[o], tm=tm)
        kv = rms_matmul(mem.reshape(b * m, d), norm_mem[layer], _mx(xa_wkv[layer]), tm=min(512, b * m),
                        out_dtype=MXU_DTYPE).reshape(b, m, 2 * d)
        x = cross_attention(x, kv, norm_xattn[layer], xa_wq[layer], xa_wo[layer], tm=tm)
        last = layer == depth - 1
        x = swiglu_ffn(x.reshape(b * t, d), norm_ffn[layer], ffn_w_gate[layer], ffn_w_up[layer], ffn_w_down[layer],
                       norm_final, tm=tm, final_norm=last).reshape(b, t, d)
    return x
```

```python
import functools
import math

import jax
import jax.numpy as jnp
from jax import lax
from jax.experimental import pallas as pl
from jax.experimental.pallas import tpu as pltpu

F32 = jnp.float32
MXU_DTYPE = jnp.bfloat16
LANES = 128
SUBLANES = 8
VMEM_LIMIT_BYTES = 60 * 1024 * 1024
NEG_BIG = -1e30

EPS = 1e-6
CONV_WIDTH = 3
M_HEADS = 4
M_HEAD_DIM = 128
G_HEADS = 4
G_KEY_DIM = 64
G_VAL_DIM = 128
G_DECAY_RANK = 16
G_DECAY_TAU = 16.0
R_HEAD_DIM = 64
R_GN_EPS = 64e-5
X_HEADS = 4

MLSTM_CHUNK = 256
GLA_CHUNK = 64
RWKV_CHUNK = 64
MLSTM_GROUP_CHUNKS = 2
GLA_GROUP_CHUNKS = 4
RWKV_BUILD_CHUNKS = 8

_EV_MQ, _EV_MK, _EV_MV, _EV_MO = 0, 4, 8, 12
_EV_GQ, _EV_GK, _EV_GV, _EV_GG, _EV_GATE = 16, 18, 20, 24, 28
_EV_BLOCKS = 29
_GATE_IG, _GATE_FG, _GATE_LR = 0, 8, 16


def _mx(a):
    return a.astype(MXU_DTYPE)


def _dot(a, b):
    return jnp.dot(_mx(a), _mx(b), preferred_element_type=F32)


def _dot_nt(a, b):
    return lax.dot_general(_mx(a), _mx(b), (((1,), (1,)), ((), ())), preferred_element_type=F32)


def _dot_tn(a, b):
    return lax.dot_general(_mx(a), _mx(b), (((0,), (0,)), ((), ())), preferred_element_type=F32)


def _scan_cumsum(mask, x, one_matmul=True):
    m = _mx(mask)
    hi = _mx(x)
    rest = x - hi.astype(F32)
    mid = _mx(rest)
    lo = _mx(rest - mid.astype(F32))
    if one_matmul:
        width = x.shape[1]
        sums = jnp.dot(m, jnp.concatenate([hi, mid, lo], axis=1), preferred_element_type=F32)
        return sums[:, :width] + sums[:, width:2 * width] + sums[:, 2 * width:]
    dot = lambda t: jnp.dot(m, t, preferred_element_type=F32)
    return dot(hi) + dot(mid) + dot(lo)


def _rms(x, g):
    return x * lax.rsqrt(jnp.mean(x * x, axis=-1, keepdims=True) + EPS) * g


def _log_sigmoid(x):
    return jnp.minimum(x, 0.0) - jnp.log1p(jnp.exp(-jnp.abs(x)))


def _silu(x):
    return x * jax.nn.sigmoid(x)


def _params(*sem):
    return pltpu.CompilerParams(dimension_semantics=sem, vmem_limit_bytes=VMEM_LIMIT_BYTES)


def _resident(shape):
    nd = len(shape)
    return pl.BlockSpec(shape, lambda *_: (0,) * nd, pipeline_mode=pl.Buffered(1))


def _scan_masks(n, reverse):
    row = lax.broadcasted_iota(jnp.int32, (n, n), 0)
    col = lax.broadcasted_iota(jnp.int32, (n, n), 1)
    if reverse:
        return col >= row, col > row
    return col <= row, col < row


def _rms_matmul_kernel(x_ref, g_ref, w_ref, o_ref):
    h = _rms(x_ref[...], g_ref[...])
    o_ref[...] = _dot(h, w_ref[...]).astype(o_ref.dtype)


def rms_matmul(x2d, g, w, *, tm, out_dtype):
    n, d = x2d.shape
    nout = w.shape[1]
    return pl.pallas_call(
        _rms_matmul_kernel,
        grid=(n // tm,),
        in_specs=[pl.BlockSpec((tm, d), lambda i: (i, 0)), _resident((1, d)), _resident((d, nout))],
        out_specs=pl.BlockSpec((tm, nout), lambda i: (i, 0)),
        out_shape=jax.ShapeDtypeStruct((n, nout), out_dtype),
        compiler_params=_params("parallel"),
        name="rms_matmul",
    )(x2d, g.reshape(1, d), w)


def _extract_col(x, lane_idx):
    lane = lax.broadcasted_iota(jnp.int32, x.shape, 1)
    return jnp.sum(jnp.where(lane == lane_idx, x, 0.0), axis=1, keepdims=True)


def _mlstm_kernel(q_ref, k_ref, v_ref, o_ref, gate_ref, conv_ref, gbias_ref, norm_ref, out_ref,
                  qc_ref, kc_ref, gl_ref, hf_ref, hb_ref, c_ref, m_ref, *, seq, chunk):
    L = chunk
    nc = seq // L
    d = M_HEAD_DIM
    head = pl.program_id(1)

    row_id = lax.broadcasted_iota(jnp.int32, (L, d), 0)
    lane_id = lax.broadcasted_iota(jnp.int32, (L, LANES), 1)
    is_fg = (lane_id >= _GATE_FG) & (lane_id < _GATE_LR)

    def conv_silu(src_ref, w, r0, c):
        x = src_ref[0, pl.ds(r0, L), :]
        prev_row = src_ref[0, pl.ds(jnp.maximum(r0 - 1, 0), 1), :] * jnp.where(c > 0, 1.0, 0.0)
        next_row = src_ref[0, pl.ds(jnp.minimum(r0 + L, seq - 1), 1), :] * jnp.where(c < nc - 1, 1.0, 0.0)
        x_prev = jnp.where(row_id == 0, prev_row, pltpu.roll(x, 1, axis=0))
        x_next = jnp.where(row_id == L - 1, next_row, pltpu.roll(x, L - 1, axis=0))
        return _silu(x_prev * w[0:1, :] + x * w[1:2, :] + x_next * w[2:3, :])

    def prologue(c, carry):
        r0 = pl.multiple_of(c * L, L)
        qc_ref[pl.ds(r0, L), :] = (conv_silu(q_ref, conv_ref[0], r0, c) * (d ** -0.5)).astype(qc_ref.dtype)
        kc_ref[pl.ds(r0, L), :] = conv_silu(k_ref, conv_ref[1], r0, c)
        g = gate_ref[0, pl.ds(r0, L), :] + gbias_ref[...]
        gl_ref[pl.ds(r0, L), :] = jnp.where(is_fg, _log_sigmoid(g), g)
        return carry

    lax.fori_loop(0, nc, prologue, 0)

    c_ref[...] = jnp.zeros_like(c_ref)
    m_ref[...] = jnp.zeros_like(m_ref)
    ones = jnp.ones((L, d), MXU_DTYPE)

    masks = [_scan_masks(L, reverse)[0] for reverse in (False, True)]
    dst_refs = (hf_ref, hb_ref)
    cb = MLSTM_GROUP_CHUNKS
    lane_sel = lax.broadcasted_iota(jnp.int32, (L, LANES), 1)

    def split_lanes(col, first):
        hi = _mx(col).astype(F32)
        mid = _mx(col - hi).astype(F32)
        lo = col - hi - mid
        out = jnp.where(lane_sel < 6, 1.0, 0.0)
        for offset, term in enumerate((hi, mid, lo)):
            out = jnp.where(lane_sel == first + offset, term, out)
        return _mx(out)

    def body(i, carry):
        chains = [(dr, j) for j in range(cb) for dr in range(2)]
        chunk_of = lambda dr, j: (i * cb + j) if dr == 0 else (nc - 1 - (i * cb + j))
        rows = [pl.ds(pl.multiple_of(chunk_of(dr, j) * L, L), L) for dr, j in chains]
        each = lambda f, *lists: [f(*args) for args in zip(*lists)]

        g = each(lambda r_: gl_ref[r_, :], rows)
        bcum = each(lambda dj, g_: _scan_cumsum(masks[dj[0]], g_, one_matmul=False), chains, g)
        ig_col = each(lambda dj, g_: _extract_col(g_, _GATE_IG + dj[0] * M_HEADS + head), chains, g)
        b_col = each(lambda dj, b_: _extract_col(b_, _GATE_FG + dj[0] * M_HEADS + head), chains, bcum)
        g_tot = each(lambda dj, b_: b_[0:1, :] if dj[0] == 1 else b_[L - 1:L, :], chains, b_col)
        dmat = each(lambda dj, b_, i_: jnp.where(masks[dj[0]], _dot_nt(split_lanes(b_, 0), split_lanes(i_ - b_, 3)),
                                                 NEG_BIG), chains, b_col, ig_col)
        dmax = each(lambda d_: jnp.max(d_, axis=1, keepdims=True), dmat)
        q = each(lambda r_: qc_ref[r_, :], rows)
        k = each(lambda r_: kc_ref[r_, :], rows)
        v_aug = each(lambda r_: jnp.concatenate([_mx(v_ref[0, r_, :]), ones], axis=1), rows)
        sc = each(lambda q_, k_, d_, m_: _mx(_dot_nt(q_, k_) * jnp.exp(d_ - m_)), q, k, dmat, dmax)
        nd_intra = each(_dot, sc, v_aug)
        kw_log = each(lambda g_, b_, i_: g_ - b_ + i_, g_tot, b_col, ig_col)
        kw_max = each(lambda w_: jnp.max(w_, axis=0, keepdims=True), kw_log)
        d_c = each(lambda k_, w_, a_, v_: _dot_tn(_mx(k_ * jnp.exp(w_ - a_)), v_), k, kw_log, kw_max, v_aug)

        c_start = [None] * len(chains)
        m_start = [None] * len(chains)
        for dr in range(2):
            c_cur = c_ref[dr]
            m_cur = m_ref[dr][:, 0:1]
            for j in range(cb):
                idx = chains.index((dr, j))
                c_start[idx], m_start[idx] = _mx(c_cur), m_cur
                m_new = jnp.maximum(g_tot[idx] + m_cur, kw_max[idx])
                c_cur = jnp.exp(g_tot[idx] + m_cur - m_new) * c_cur + jnp.exp(kw_max[idx] - m_new) * d_c[idx]
                m_cur = m_new
            c_ref[dr] = c_cur
            m_ref[dr] = jnp.broadcast_to(m_cur, (1, LANES))

        nd_inter = each(_dot, q, c_start)
        for idx, (dr, j) in enumerate(chains):
            inter_log = b_col[idx] + m_start[idx]
            m_t = jnp.maximum(inter_log, dmax[idx])
            numden = jnp.exp(dmax[idx] - m_t) * nd_intra[idx] + jnp.exp(inter_log - m_t) * nd_inter[idx]
            dst_refs[dr][rows[idx], :] = numden[:, :d] / jnp.maximum(jnp.abs(numden[:, d:]), jnp.exp(-m_t))
        return carry

    lax.fori_loop(0, nc // cb, body, 0)

    def epilogue(c, carry):
        r0 = pl.multiple_of(c * L, L)
        h = hf_ref[pl.ds(r0, L), :] + hb_ref[pl.ds(r0, L), :]
        h = _rms(h, norm_ref[...])
        out_ref[0, pl.ds(r0, L), :] = h * jax.nn.sigmoid(o_ref[0, pl.ds(r0, L), :])
        return carry

    lax.fori_loop(0, nc, epilogue, 0)


def mlstm_mixer(p, conv_qk, gate_bias, m_norm, *, chunk):
    b, t, _ = p.shape
    d = M_HEAD_DIM
    blk = lambda off: pl.BlockSpec((1, t, d), lambda i, h, off=off: (i, 0, off + h))
    conv = conv_qk.reshape(CONV_WIDTH, 2, M_HEADS, d).transpose(2, 1, 0, 3)
    return pl.pallas_call(
        functools.partial(_mlstm_kernel, seq=t, chunk=chunk),
        grid=(b, M_HEADS),
        in_specs=[blk(_EV_MQ), blk(_EV_MK), blk(_EV_MV), blk(_EV_MO),
                  pl.BlockSpec((1, t, LANES), lambda i, h: (i, 0, _EV_GATE)),
                  pl.BlockSpec((None, 2, CONV_WIDTH, d), lambda i, h: (h, 0, 0, 0)),
                  pl.BlockSpec((1, LANES), lambda i, h: (0, 0)),
                  pl.BlockSpec((1, d), lambda i, h: (0, h))],
        out_specs=pl.BlockSpec((1, t, d), lambda i, h: (i, 0, h)),
        out_shape=jax.ShapeDtypeStruct((b, t, M_HEADS * d), F32),
        scratch_shapes=[pltpu.VMEM((t, d), MXU_DTYPE),
                        pltpu.VMEM((t, d), F32),
                        pltpu.VMEM((t, LANES), F32),
                        pltpu.VMEM((t, d), F32),
                        pltpu.VMEM((t, d), F32),
                        pltpu.VMEM((2, d, 2 * d), F32),
                        pltpu.VMEM((2, 1, LANES), F32)],
        compiler_params=_params("parallel", "parallel"),
        name="mlstm",
    )(p, p, p, p, p, conv, gate_bias, m_norm.reshape(1, M_HEADS * d))


def _gla_kernel(q_ref, k_ref, v0_ref, v1_ref, g0_ref, g1_ref, gate_ref, w2_ref, b2_ref, norm_ref, out_ref,
                of_ref, ob_ref, s_ref, *, seq, chunk):
    L = chunk
    nc = seq // L
    dk2 = 2 * G_KEY_DIM
    dv = G_VAL_DIM
    lane_k = lax.broadcasted_iota(jnp.int32, (L, dk2), 1)
    head0 = (lane_k < G_KEY_DIM).astype(F32)
    head1 = 1.0 - head0
    s_row = lax.broadcasted_iota(jnp.int32, (2 * dv, dk2), 0)
    s_col = lax.broadcasted_iota(jnp.int32, (2 * dv, dk2), 1)
    s_mask = ((s_row < dv) == (s_col < G_KEY_DIM)).astype(F32)

    s_ref[...] = jnp.zeros_like(s_ref)
    masks = [_scan_masks(L, reverse)[0] for reverse in (False, True)]
    masks2 = [jnp.concatenate([m, m], axis=0) for m in masks]
    dst_refs = (of_ref, ob_ref)
    cb = GLA_GROUP_CHUNKS

    def body(i, carry):
        chains = [(d, j) for j in range(cb) for d in range(2)]
        chunk_of = lambda d, j: (i * cb + j) if d == 0 else (nc - 1 - (i * cb + j))
        rows = [pl.ds(pl.multiple_of(chunk_of(d, j) * L, L), L) for d, j in chains]
        each = lambda f, *lists: [f(*args) for args in zip(*lists)]

        z = each(lambda dj, r_: _dot(gate_ref[0, r_, :], w2_ref[dj[0], 0]) + b2_ref[dj[0], 0], chains, rows)
        la = each(lambda z_: _log_sigmoid(z_) * (1.0 / G_DECAY_TAU), z)
        bcum = each(lambda dj, l_: _scan_cumsum(masks[dj[0]], l_), chains, la)
        g_tot = each(lambda dj, b_: b_[0:1, :] if dj[0] == 1 else b_[L - 1:L, :], chains, bcum)
        mid = each(lambda b_: b_[L // 2:L // 2 + 1, :], bcum)
        q = each(lambda r_: q_ref[0, r_, :] * (G_KEY_DIM ** -0.5), rows)
        k = each(lambda r_: k_ref[0, r_, :], rows)
        v = each(lambda r_: jnp.concatenate([_mx(v0_ref[0, r_, :]), _mx(v1_ref[0, r_, :])], axis=1), rows)
        q_in = each(lambda q_, b_, m_: q_ * jnp.exp(b_ - m_), q, bcum, mid)
        q_in2 = each(lambda q_: _mx(jnp.concatenate([q_ * head0, q_ * head1], axis=0)), q_in)
        k_in = each(lambda k_, b_, m_: _mx(k_ * jnp.exp(m_ - b_)), k, bcum, mid)
        scores = each(lambda dj, q_, k_: _mx(jnp.where(masks2[dj[0]], _dot_nt(q_, k_), 0.0)), chains, q_in2, k_in)
        o_intra = each(lambda a_, v_: jnp.concatenate([_dot(a_[:L], v_[:, :dv]), _dot(a_[L:], v_[:, dv:])], axis=1),
                       scores, v)
        q_hat = each(lambda q_, b_: _mx(q_ * jnp.exp(b_)), q, bcum)
        k_hat = each(lambda k_, b_, g_: _mx(k_ * jnp.exp(g_ - b_)), k, bcum, g_tot)
        d_s = each(lambda v_, k_: _dot_tn(v_, k_) * s_mask, v, k_hat)
        decay = each(jnp.exp, g_tot)

        s_at_start = [None] * len(chains)
        for d in range(2):
            s_cur = s_ref[d]
            for j in range(cb):
                idx = chains.index((d, j))
                s_at_start[idx] = _mx(s_cur)
                s_cur = s_cur * decay[idx] + d_s[idx]
            s_ref[d] = s_cur
        o_inter = each(_dot_nt, q_hat, s_at_start)
        for idx, (d, j) in enumerate(chains):
            dst_refs[d][rows[idx], :] = o_intra[idx] + o_inter[idx]
        return carry

    lax.fori_loop(0, nc // cb, body, 0)

    rows_out = L * cb

    def epilogue(c, carry):
        rows = pl.ds(pl.multiple_of(c * rows_out, rows_out), rows_out)
        o = of_ref[rows, :] + ob_ref[rows, :]
        g = jnp.concatenate([g0_ref[0, rows, :], g1_ref[0, rows, :]], axis=1)
        nrm = norm_ref[...]
        o = jnp.concatenate([_rms(o[:, :dv], nrm[:, :dv]), _rms(o[:, dv:], nrm[:, dv:])], axis=1)
        out_ref[0, rows, :] = o * _silu(g)
        return carry

    lax.fori_loop(0, nc // cb, epilogue, 0)


def gla_mixer(p, decay_w2, decay_b, g_norm, *, chunk):
    b, t, _ = p.shape
    dv = G_VAL_DIM
    pairs = G_HEADS // 2
    dk2 = 2 * G_KEY_DIM
    w2 = jnp.zeros((2, LANES, pairs * dk2), F32)
    for direction in range(2):
        lo = _GATE_LR + direction * G_DECAY_RANK
        w2 = w2.at[direction, lo:lo + G_DECAY_RANK].set(decay_w2[direction])
    w2 = _mx(w2.reshape(2, LANES, pairs, dk2).transpose(0, 2, 1, 3))
    b2 = decay_b.reshape(2, pairs, 1, dk2)
    blk = lambda off, mult: pl.BlockSpec((1, t, LANES), lambda i, h, off=off, mult=mult: (i, 0, off + mult * h))
    return pl.pallas_call(
        functools.partial(_gla_kernel, seq=t, chunk=chunk),
        grid=(b, pairs),
        in_specs=[blk(_EV_GQ, 1), blk(_EV_GK, 1), blk(_EV_GV, 2), blk(_EV_GV + 1, 2),
                  blk(_EV_GG, 2), blk(_EV_GG + 1, 2),
                  pl.BlockSpec((1, t, LANES), lambda i, h: (i, 0, _EV_GATE)),
                  pl.BlockSpec((2, 1, LANES, dk2), lambda i, h: (0, h, 0, 0)),
                  pl.BlockSpec((2, 1, 1, dk2), lambda i, h: (0, h, 0, 0)),
                  pl.BlockSpec((1, 2 * dv), lambda i, h: (0, h))],
        out_specs=pl.BlockSpec((1, t, 2 * dv), lambda i, h: (i, 0, h)),
        out_shape=jax.ShapeDtypeStruct((b, t, G_HEADS * dv), F32),
        scratch_shapes=[pltpu.VMEM((t, 2 * dv), F32),
                        pltpu.VMEM((t, 2 * dv), F32),
                        pltpu.VMEM((2, 2 * dv, dk2), F32)],
        compiler_params=_params("parallel", "parallel"),
        name="gla",
    )(p, p, p, p, p, p, p, w2, b2, g_norm.reshape(1, G_HEADS * dv))


def _rwkv_in_kernel(x_ref, xp_ref, xn_ref, nrm_ref, mu_ref, wrkv_ref, w1_ref, w2_ref, w0_ref,
                    a1_ref, a2_ref, a0_ref, g1_ref, g2_ref,
                    r_ref, k_ref, v_ref, a_ref, lwf_ref, lwb_ref, g_ref, *, tm):
    i = pl.program_id(1)
    nt = pl.num_programs(1)
    nrm = nrm_ref[...]
    h = _rms(x_ref[0], nrm)
    before = _rms(xp_ref[0, 0], nrm)[SUBLANES - 1:SUBLANES, :] * jnp.where(i > 0, 1.0, 0.0)
    after = _rms(xn_ref[0, 0], nrm)[0:1, :] * jnp.where(i < nt - 1, 1.0, 0.0)
    row = lax.broadcasted_iota(jnp.int32, h.shape, 0)
    h_prev = jnp.where(row == 0, before, pltpu.roll(h, 1, axis=0))
    h_next = jnp.where(row == tm - 1, after, pltpu.roll(h, tm - 1, axis=0))
    hh = 0.5 * (h_prev + h_next) - h
    mix = lambda j: h + hh * mu_ref[j:j + 1, :]
    xr, xw, xk, xv, xa, xg = (mix(j) for j in range(6))
    r_ref[0] = _dot(xr, wrkv_ref[0]).astype(r_ref.dtype)
    k_ref[0] = _dot(xk, wrkv_ref[1]).astype(k_ref.dtype)
    v_ref[0] = _dot(xv, wrkv_ref[2]).astype(v_ref.dtype)
    a_ref[0] = jax.nn.sigmoid(a0_ref[...] + _dot(_dot(xa, a1_ref[...]), a2_ref[...])).astype(a_ref.dtype)
    g_ref[0] = _dot(jax.nn.sigmoid(_dot(xg, g1_ref[...])), g2_ref[...]).astype(g_ref.dtype)
    for direction, dst in ((0, lwf_ref), (1, lwb_ref)):
        z = w0_ref[direction:direction + 1, :] + _dot(jnp.tanh(_dot(xw, w1_ref[direction])), w2_ref[direction])
        dst[0] = jax.nn.sigmoid(z) * -math.exp(-0.5)


def rwkv_in(x, norm, mu, w_rkv, w0, w1, w2, a0, a1, a2, g1, g2, *, tm):
    b, t, d = x.shape
    x8 = x.reshape(b, t // SUBLANES, SUBLANES, d)
    tb = tm // SUBLANES
    nb = t // SUBLANES
    rank_g = g1.shape[1]
    pad_g = -rank_g % LANES
    g1p = jnp.pad(g1, ((0, 0), (0, pad_g)))
    g2p = jnp.pad(g2, ((0, pad_g), (0, 0)))
    tile = pl.BlockSpec((1, tm, d), lambda i, j: (i, j, 0))
    out = [jax.ShapeDtypeStruct((b, t, d), dt) for dt in (MXU_DTYPE,) * 4 + (F32, F32, MXU_DTYPE)]
    weights = [norm.reshape(1, d), mu, _mx(w_rkv), _mx(w1), _mx(w2), w0,
               _mx(a1), _mx(a2), a0.reshape(1, d), _mx(g1p), _mx(g2p)]
    return pl.pallas_call(
        functools.partial(_rwkv_in_kernel, tm=tm),
        grid=(b, t // tm),
        in_specs=[tile,
                  pl.BlockSpec((1, 1, SUBLANES, d), lambda i, j: (i, jnp.maximum(j * tb - 1, 0), 0, 0)),
                  pl.BlockSpec((1, 1, SUBLANES, d), lambda i, j: (i, jnp.minimum((j + 1) * tb, nb - 1), 0, 0))]
                 + [_resident(w.shape) for w in weights],
        out_specs=[tile] * 7,
        out_shape=out,
        compiler_params=_params("parallel", "parallel"),
        name="rwkv_in",
    )(x, x8, x8, *weights)


def _rwkv_scan_kernel(r_ref, k_ref, v_ref, a_ref, lwf_ref, lwb_ref, kk_ref, ka_ref, rk_ref, lnw_ref, lnb_ref,
                      out_ref, y_ref, rq_ref, m_ref, sadd_ref, gam_ref, s_ref, *, seq, chunk):
    L = chunk
    nc = seq // L
    n = R_HEAD_DIM
    lane = lax.broadcasted_iota(jnp.int32, (L, 2 * n), 1)
    head0 = (lane < n).astype(F32)
    head1 = 1.0 - head0

    def head_sum(x):
        s0 = jnp.sum(x * head0, axis=1, keepdims=True)
        s1 = jnp.sum(x * head1, axis=1, keepdims=True)
        return s0 * head0 + s1 * head1

    head0_mx, head1_mx = _mx(head0), _mx(head1)

    def stack(x):
        x = _mx(x)
        return jnp.concatenate([x * head0_mx, x * head1_mx], axis=0)

    def load(c):
        r0 = pl.multiple_of(c * L, L)
        rows = pl.ds(r0, L)
        r, k, v, a = (ref[0, rows, :].astype(F32) for ref in (r_ref, k_ref, v_ref, a_ref))
        kk = k * kk_ref[...]
        kk = kk / jnp.maximum(jnp.sqrt(head_sum(kk * kk)), 1e-12)
        kp = k * (1.0 + (a - 1.0) * ka_ref[...])
        return rows, r, kp, v, -kk, kk * a

    lane2 = lax.broadcasted_iota(jnp.int32, (L, 4 * n), 1)
    head0_2 = ((lane2 & (2 * n - 1)) < n).astype(F32)
    head0_2_mx, head1_2_mx = _mx(head0_2), _mx(1.0 - head0_2)

    def stack2(x):
        x = _mx(x)
        return jnp.concatenate([x * head0_2_mx, x * head1_2_mx], axis=0)

    masks_l = [_scan_masks(L, reverse)[0] for reverse in (False, True)]
    row_cat = lax.broadcasted_iota(jnp.int32, (L, 2 * L), 0)
    col_cat = lax.broadcasted_iota(jnp.int32, (L, 2 * L), 1) & (L - 1)
    masks = [(col_cat <= row_cat, col_cat < row_cat), (col_cat >= row_cat, col_cat > row_cat)]
    eye_cat = (col_cat == row_cat).astype(F32)
    same_head = ((lax.broadcasted_iota(jnp.int32, (2 * n, 2 * n), 0) < n)
                 == (lax.broadcasted_iota(jnp.int32, (2 * n, 2 * n), 1) < n)).astype(F32)
    lw_refs = (lwf_ref, lwb_ref)
    cb = min(RWKV_BUILD_CHUNKS, nc // 2)
    groups = nc // cb
    chunk_of = lambda d, g, j: (g * cb + j) if d == 0 else (nc - 1 - (g * cb + j))
    each = lambda f, *lists: [f(*args) for args in zip(*lists)]

    def finish(c, y):
        rows = pl.ds(pl.multiple_of(c * L, L), L)
        r, k, v, a = (ref[0, rows, :].astype(F32) for ref in (r_ref, k_ref, v_ref, a_ref))
        kp = k * (1.0 + (a - 1.0) * ka_ref[...])
        mean = head_sum(y) * (1.0 / n)
        yc = y - mean
        var = head_sum(yc * yc) * (1.0 / n)
        yn = yc * lax.rsqrt(var + R_GN_EPS) * lnw_ref[...] + lnb_ref[...]
        out_ref[0, rows, :] = yn + head_sum(r * kp * rk_ref[...]) * v

    def sweep_steps(g, crossed):
        state = [s_ref[d] for d in range(2)]

        def step(j):
            cs = [chunk_of(d, g, j) for d in range(2)]
            rows = [pl.ds(pl.multiple_of(c * L, L), L) for c in cs]
            s_mx = each(_mx, state)
            y = [y_ref[d, rows[d], :] + _dot_nt(rq_ref[d, rows[d], :], s_mx[d]) for d in range(2)]
            for d in range(2):
                state[d] = (state[d] * gam_ref[d, pl.ds(cs[d], 1), :] + _dot(s_mx[d], m_ref[d, cs[d]])
                            + sadd_ref[d, cs[d]])
            if j == cb - 1:
                for d in range(2):
                    s_ref[d] = state[d]
            for d in range(2):
                if crossed:
                    finish(cs[d], y[d] + y_ref[1 - d, rows[d], :])
                else:
                    y_ref[d, rows[d], :] = y[d]

        return [functools.partial(step, j) for j in range(cb)]

    def build_group(g, interleaved=()):
        pending = list(interleaved)
        run_one = lambda: pending.pop(0)() if pending else None
        chains = [(d, j) for j in range(cb) for d in range(2)]
        chunk_ids = [chunk_of(d, g, j) for d, j in chains]
        loaded = [load(c) for c in chunk_ids]

        lw = each(lambda dj, ld: lw_refs[dj[0]][0, ld[0], :], chains, loaded)
        cin = each(lambda dj, x: _scan_cumsum(masks_l[dj[0]], x), chains, lw)
        ctot = each(lambda dj, c_: c_[0:1, :] if dj[0] == 1 else c_[L - 1:L, :], chains, cin)
        e_out = each(lambda c_: jnp.exp(-c_), cin)
        e_end = each(lambda c_, t_: jnp.exp(t_ - c_), cin, ctot)
        a_hat = each(lambda ld, c_, l_: ld[4] * jnp.exp(c_ - l_), loaded, cin, lw)
        r_hat = each(lambda ld, c_: ld[1] * jnp.exp(c_), loaded, cin)
        x = each(lambda a_, r_: _mx(jnp.concatenate([a_, r_], axis=0)), a_hat, r_hat)
        yk = each(lambda ld, e_: jnp.concatenate([stack(ld[5] * e_), stack(ld[2] * e_)], axis=0), loaded, e_out)
        run_one()
        gram = each(_dot_nt, x, yk)
        a_ab = each(lambda dj, g_: jnp.where(masks[dj[0]][1], g_[:L, :2 * L], 0.0), chains, gram)
        a_ak = each(lambda dj, g_: _mx(jnp.where(masks[dj[0]][1], g_[:L, 2 * L:], 0.0)), chains, gram)
        a_rb = each(lambda dj, g_: _mx(jnp.where(masks[dj[0]][0], g_[L:, :2 * L], 0.0)), chains, gram)
        a_rk = each(lambda dj, g_: _mx(jnp.where(masks[dj[0]][0], g_[L:, 2 * L:], 0.0)), chains, gram)

        t_inv = each(lambda n_: eye_cat + n_, a_ab)
        power = each(lambda n_: _mx(_dot(n_, stack(n_))), a_ab)
        for _ in range(L.bit_length() - 3):
            both = each(lambda p_, t_: _dot(p_, stack2(jnp.concatenate([p_, _mx(t_)], axis=1))), power, t_inv)
            power = each(lambda b_: _mx(b_[:, :2 * L]), both)
            t_inv = each(lambda t_, b_: t_ + b_[:, 2 * L:], t_inv, both)
            run_one()
        t_inv = each(lambda p_, t_: t_ + _dot(p_, stack(t_)), power, t_inv)

        v2 = each(lambda ld: stack(ld[3]), loaded)
        av = each(_dot, a_ak, v2)
        w = each(lambda t_, a_, av_: _dot(t_, stack2(jnp.concatenate([_mx(a_), _mx(av_)], axis=1))),
                 t_inv, a_hat, av)
        z = each(lambda b_, w_: _dot(b_, stack2(w_)), a_rb, w)
        y0 = each(lambda z_, k_, v_: z_[:, 2 * n:] + _dot(k_, v_), z, a_rk, v2)
        while pending:
            run_one()
        b_end = each(lambda ld, e_: _mx(ld[5] * e_), loaded, e_end)
        k_end = each(lambda ld, e_: _mx(ld[2] * e_), loaded, e_end)
        m_lr = each(lambda w_, b_: _dot_tn(w_[:, :2 * n], b_) * same_head, w, b_end)
        s_add = each(lambda w_, ld, b_, k_: _dot_tn(jnp.concatenate([w_[:, 2 * n:], ld[3]], axis=0),
                                                    jnp.concatenate([b_, k_], axis=0)) * same_head,
                     w, loaded, b_end, k_end)
        for idx, (d, j) in enumerate(chains):
            rows = loaded[idx][0]
            y_ref[d, rows, :] = y0[idx]
            rq_ref[d, rows, :] = (r_hat[idx] + z[idx][:, :2 * n]).astype(rq_ref.dtype)
            m_ref[d, chunk_ids[idx]] = m_lr[idx].astype(m_ref.dtype)
            sadd_ref[d, chunk_ids[idx]] = s_add[idx]
            gam_ref[d, pl.ds(chunk_ids[idx], 1), :] = jnp.exp(ctot[idx])

    s_ref[...] = jnp.zeros_like(s_ref)
    build_group(0)

    def before_crossing(g, carry):
        build_group(g, sweep_steps(g - 1, crossed=False))
        return carry

    def after_crossing(g, carry):
        build_group(g, sweep_steps(g - 1, crossed=True))
        return carry

    lax.fori_loop(1, groups // 2 + 1, before_crossing, 0)
    lax.fori_loop(groups // 2 + 1, groups, after_crossing, 0)
    for step in sweep_steps(groups - 1, crossed=groups > 1):
        step()


def rwkv_scan(r, k, v, a, lwf, lwb, k_k, k_a, r_k, ln_w, ln_b, *, chunk):
    b, t, d = r.shape
    w = 2 * R_HEAD_DIM
    pairs = d // w
    nc = t // chunk
    assert chunk == R_HEAD_DIM and nc % 2 == 0, (chunk, t)
    blk = pl.BlockSpec((1, t, w), lambda i, h: (i, 0, h))
    vec = pl.BlockSpec((1, w), lambda i, h: (0, h))
    row = lambda z: z.reshape(1, d)
    return pl.pallas_call(
        functools.partial(_rwkv_scan_kernel, seq=t, chunk=chunk),
        grid=(b, pairs),
        in_specs=[blk] * 6 + [vec] * 5,
        out_specs=blk,
        out_shape=jax.ShapeDtypeStruct((b, t, d), F32),
        scratch_shapes=[pltpu.VMEM((2, t, w), F32),
                        pltpu.VMEM((2, t, w), MXU_DTYPE),
                        pltpu.VMEM((2, nc, w, w), MXU_DTYPE),
                        pltpu.VMEM((2, nc, w, w), F32),
                        pltpu.VMEM((2, nc, w), F32),
                        pltpu.VMEM((2, w, w), F32)],
        compiler_params=_params("parallel", "parallel"),
        name="rwkv_scan",
    )(r, k, v, a, lwf, lwb, row(k_k), row(k_a), row(r_k), row(ln_w), row(ln_b))


def _xattn_kernel(x_ref, a_ref, b_ref, wm_ref, nrm_ref, wq_ref, k_ref, v_ref, wo_ref, o_ref, *, gated):
    if gated:
        merged = a_ref[0] * b_ref[0].astype(F32)
    else:
        merged = jnp.concatenate([_mx(a_ref[0]), _mx(b_ref[0])], axis=1)
    x = x_ref[0] + _dot(merged, wm_ref[...])
    d = x.shape[-1]
    dh = d // X_HEADS
    q = _mx(_dot(_rms(x, nrm_ref[...]), wq_ref[...]))
    cols = [slice(hd * dh, (hd + 1) * dh) for hd in range(X_HEADS)]
    s = [_dot_nt(q[:, c], k_ref[0, :, c]) * (dh ** -0.5) for c in cols]
    p = [jnp.exp(s_ - jnp.max(s_, axis=-1, keepdims=True)) for s_ in s]
    p = [_mx(p_ / jnp.sum(p_, axis=-1, keepdims=True)) for p_ in p]
    heads = [_mx(_dot(p_, v_ref[0, :, c])) for p_, c in zip(p, cols)]
    o_ref[0] = x + _dot(jnp.concatenate(heads, axis=1), wo_ref[...])


def mixer_out_cross_attention(x, a, b_in, w_mix, kv, norm, wq, wo, *, tm, gated):
    b, t, d = x.shape
    m = kv.shape[1]
    tile = lambda width: pl.BlockSpec((1, tm, width), lambda i, j: (i, j, 0))
    return pl.pallas_call(
        functools.partial(_xattn_kernel, gated=gated),
        grid=(b, t // tm),
        in_specs=[tile(d), tile(a.shape[-1]), tile(b_in.shape[-1]), _resident(w_mix.shape),
                  _resident((1, d)), _resident((d, d)),
                  pl.BlockSpec((1, m, d), lambda i, j: (i, 0, 0)),
                  pl.BlockSpec((1, m, d), lambda i, j: (i, 0, 1)),
                  _resident((d, d))],
        out_specs=tile(d),
        out_shape=jax.ShapeDtypeStruct((b, t, d), F32),
        compiler_params=_params("parallel", "parallel"),
        name="mixer_out_cross_attention",
    )(x, a, b_in, _mx(w_mix), norm.reshape(1, d), _mx(wq), kv, kv, _mx(wo))


def _ffn_kernel(x_ref, nrm_ref, wg_ref, wu_ref, wd_ref, fin_ref, o_ref, *, final_norm):
    x = x_ref[...]
    h = _mx(_rms(x, nrm_ref[...]))
    act = _silu(_dot(h, wg_ref[...])) * _dot(h, wu_ref[...])
    y = x + _dot(act, wd_ref[...])
    o_ref[...] = _rms(y, fin_ref[...]) if final_norm else y


def swiglu_ffn(x2d, norm, w_gate, w_up, w_down, final_gain, *, tm, final_norm):
    n, d = x2d.shape
    dff = w_gate.shape[1]
    row = pl.BlockSpec((tm, d), lambda i: (i, 0))
    return pl.pallas_call(
        functools.partial(_ffn_kernel, final_norm=final_norm),
        grid=(n // tm,),
        in_specs=[row, _resident((1, d)), _resident((d, dff)), _resident((d, dff)), _resident((dff, d)),
                  _resident((1, d))],
        out_specs=row,
        out_shape=jax.ShapeDtypeStruct((n, d), F32),
        compiler_params=_params("parallel"),
        name="swiglu_ffn",
    )(x2d, norm.reshape(1, d), _mx(w_gate), _mx(w_up), _mx(w_down), final_gain.reshape(1, d))


def _pack_even_in(w_in):
    d = w_in.shape[0]
    mw, gkw, gvw = M_HEADS * M_HEAD_DIM, G_HEADS * G_KEY_DIM, G_HEADS * G_VAL_DIM
    widths = (mw, mw, mw, mw, 2 * M_HEADS, 2 * M_HEADS, gkw, gkw, gvw, gvw, 2 * G_DECAY_RANK)
    starts = [0]
    for wdt in widths:
        starts.append(starts[-1] + wdt)
    mq, mk, mv, mo, mi, mf, gq, gk, gv, gg, glr = (w_in[:, starts[j]:starts[j + 1]] for j in range(len(widths)))
    gate = jnp.concatenate([mi, mf, glr], axis=1)
    gate = jnp.pad(gate, ((0, 0), (0, LANES - gate.shape[1])))
    packed = jnp.concatenate([mq, mk, mv, mo, gq, gk, gv, gg, gate], axis=1)
    assert packed.shape == (d, _EV_BLOCKS * LANES)
    return _mx(packed)


def _even_mixer_layer(x, norm, w_in, conv_qk, ig_bias, fg_bias, m_norm, decay_w2, decay_b, g_norm, *, tm):
    b, t, d = x.shape
    x2d = x.reshape(b * t, d)
    p = rms_matmul(x2d, norm, _pack_even_in(w_in), tm=tm, out_dtype=F32).reshape(b, t, -1)
    gate_bias = jnp.concatenate([ig_bias.reshape(-1), fg_bias.reshape(-1)])
    gate_bias = jnp.pad(gate_bias, (0, LANES - gate_bias.shape[0])).reshape(1, LANES)
    hm = mlstm_mixer(p, conv_qk, gate_bias, m_norm, chunk=min(MLSTM_CHUNK, t))
    og = gla_mixer(p, decay_w2, decay_b, g_norm, chunk=min(GLA_CHUNK, t))
    return hm, og


def _rwkv_mixer_layer(x, norm, mu, w_rkv, w0, w1, w2, a0, a1, a2, g1, g2, k_k, k_a, r_k, ln_w, ln_b, *, tm):
    b, t, d = x.shape
    r, k, v, a, lwf, lwb, g = rwkv_in(x, norm, mu, w_rkv, w0, w1, w2, a0, a1, a2, g1, g2, tm=tm)
    y = rwkv_scan(r, k, v, a, lwf, lwb, k_k, k_a, r_k.reshape(-1), ln_w, ln_b, chunk=min(RWKV_CHUNK, t))
    return y, g


def kernel(x, mem, norm_mix, norm_xattn, norm_mem, norm_ffn, norm_final, xa_wq, xa_wkv, xa_wo, ffn_w_gate, ffn_w_up, ffn_w_down, ev_w_in, ev_conv_qk, ev_m_ig_bias, ev_m_fg_bias, ev_m_norm, ev_g_decay_w2, ev_g_decay_b, ev_g_norm, ev_w_out, od_mu, od_w_rkv, od_w0, od_w1, od_w2, od_a0, od_a1, od_a2, od_g1, od_g2, od_k_k, od_k_a, od_r_k, od_ln_w, od_ln_b, od_w_o):
    b, t, d = x.shape
    m = mem.shape[1]
    depth = norm_mix.shape[0]
    tm = min(512, t)
    for layer in range(depth):
        if layer % 2 == 0:
            e = layer // 2
            mix_a, mix_b = _even_mixer_layer(x, norm_mix[layer], ev_w_in[e], ev_conv_qk[e], ev_m_ig_bias[e],
                                             ev_m_fg_bias[e], ev_m_norm[e], ev_g_decay_w2[e], ev_g_decay_b[e],
                                             ev_g_norm[e], tm=tm)
            w_mix = ev_w_out[e]
        else:
            o = layer // 2
            mix_a, mix_b = _rwkv_mixer_layer(x, norm_mix[layer], od_mu[o], od_w_rkv[o], od_w0[o], od_w1[o], od_w2[o],
                                             od_a0[o], od_a1[o], od_a2[o], od_g1[o], od_g2[o], od_k_k[o], od_k_a[o],
                                             od_r_k[o], od_ln_w[o], od_ln_b[o], tm=tm)
            w_mix = od_w_o[o]
        kv = rms_matmul(mem.reshape(b * m, d), norm_mem[layer], _mx(xa_wkv[layer]), tm=min(512, b * m),
                        out_dtype=MXU_DTYPE).reshape(b, m, 2 * d)
        x = mixer_out_cross_attention(x, mix_a, mix_b, w_mix, kv, norm_xattn[layer], xa_wq[layer], xa_wo[layer],
                                      tm=tm, gated=layer % 2 == 1)
        last = layer == depth - 1
        x = swiglu_ffn(x.reshape(b * t, d), norm_ffn[layer], ffn_w_gate[layer], ffn_w_up[layer], ffn_w_down[layer],
                       norm_final, tm=tm, final_norm=last).reshape(b, t, d)
    return x
```

```python
import functools
import math

import jax
import jax.numpy as jnp
from jax import lax
from jax.experimental import pallas as pl
from jax.experimental.pallas import tpu as pltpu

F32 = jnp.float32
MXU_DTYPE = jnp.bfloat16
LANES = 128
SUBLANES = 8
VMEM_LIMIT_BYTES = 60 * 1024 * 1024
NEG_BIG = -1e30

EPS = 1e-6
CONV_WIDTH = 3
M_HEADS = 4
M_HEAD_DIM = 128
G_HEADS = 4
G_KEY_DIM = 64
G_VAL_DIM = 128
G_DECAY_RANK = 16
G_DECAY_TAU = 16.0
R_HEAD_DIM = 64
R_GN_EPS = 64e-5
X_HEADS = 4

MLSTM_CHUNK = 256
GLA_CHUNK = 64
RWKV_CHUNK = 64
MLSTM_GROUP_CHUNKS = 2
GLA_GROUP_CHUNKS = 8
RWKV_BUILD_CHUNKS = 8

_EV_MQ, _EV_MK, _EV_MV, _EV_MO = 0, 4, 8, 12
_EV_GQ, _EV_GK, _EV_GV, _EV_GG, _EV_GATE = 16, 18, 20, 24, 28
_EV_BLOCKS = 29
_GATE_IG, _GATE_FG, _GATE_LR = 0, 8, 16


def _mx(a):
    return a.astype(MXU_DTYPE)


def _dot(a, b):
    return jnp.dot(_mx(a), _mx(b), preferred_element_type=F32)


def _dot_nt(a, b):
    return lax.dot_general(_mx(a), _mx(b), (((1,), (1,)), ((), ())), preferred_element_type=F32)


def _dot_tn(a, b):
    return lax.dot_general(_mx(a), _mx(b), (((0,), (0,)), ((), ())), preferred_element_type=F32)


def _scan_cumsum(mask, x, one_matmul=True):
    m = _mx(mask)
    hi = _mx(x)
    rest = x - hi.astype(F32)
    mid = _mx(rest)
    lo = _mx(rest - mid.astype(F32))
    if one_matmul:
        width = x.shape[1]
        sums = jnp.dot(m, jnp.concatenate([hi, mid, lo], axis=1), preferred_element_type=F32)
        return sums[:, :width] + sums[:, width:2 * width] + sums[:, 2 * width:]
    dot = lambda t: jnp.dot(m, t, preferred_element_type=F32)
    return dot(hi) + dot(mid) + dot(lo)


def _rms(x, g):
    return x * lax.rsqrt(jnp.mean(x * x, axis=-1, keepdims=True) + EPS) * g


def _log_sigmoid(x):
    return jnp.minimum(x, 0.0) - jnp.log1p(jnp.exp(-jnp.abs(x)))


def _silu(x):
    return x * jax.nn.sigmoid(x)


def _params(*sem):
    return pltpu.CompilerParams(dimension_semantics=sem, vmem_limit_bytes=VMEM_LIMIT_BYTES)


def _resident(shape):
    nd = len(shape)
    return pl.BlockSpec(shape, lambda *_: (0,) * nd, pipeline_mode=pl.Buffered(1))


def _scan_masks(n, reverse):
    row = lax.broadcasted_iota(jnp.int32, (n, n), 0)
    col = lax.broadcasted_iota(jnp.int32, (n, n), 1)
    if reverse:
        return col >= row, col > row
    return col <= row, col < row


def _rms_matmul_kernel(x_ref, g_ref, w_ref, o_ref):
    h = _rms(x_ref[...], g_ref[...])
    o_ref[...] = _dot(h, w_ref[...]).astype(o_ref.dtype)


def rms_matmul(x2d, g, w, *, tm, out_dtype):
    n, d = x2d.shape
    nout = w.shape[1]
    return pl.pallas_call(
        _rms_matmul_kernel,
        grid=(n // tm,),
        in_specs=[pl.BlockSpec((tm, d), lambda i: (i, 0)), _resident((1, d)), _resident((d, nout))],
        out_specs=pl.BlockSpec((tm, nout), lambda i: (i, 0)),
        out_shape=jax.ShapeDtypeStruct((n, nout), out_dtype),
        compiler_params=_params("parallel"),
        name="rms_matmul",
    )(x2d, g.reshape(1, d), w)


def _even_in_kernel(x_ref, g_ref, w_ref, wgt_ref, brow_ref, bcol_ref, o_ref, gt_ref):
    h = _mx(_rms(x_ref[...], g_ref[...]))
    p = _dot(h, w_ref[...])
    main = (_EV_BLOCKS - 1) * LANES
    o_ref[:, :main] = p[:, :main]
    gates = p[:, main:] + brow_ref[...]
    lane = lax.broadcasted_iota(jnp.int32, gates.shape, 1)
    o_ref[:, main:] = jnp.where((lane >= _GATE_FG) & (lane < _GATE_LR), _log_sigmoid(gates), gates)
    gates_t = _dot_nt(wgt_ref[...], h) + bcol_ref[...]
    row = lax.broadcasted_iota(jnp.int32, gates_t.shape, 0)
    gt_ref[...] = jnp.where(row >= _GATE_FG, _log_sigmoid(gates_t), gates_t)


def even_in_proj(x2d, g, w_packed, w_gates_t, gate_bias, *, tm):
    n, d = x2d.shape
    nout = w_packed.shape[1]
    ngate = w_gates_t.shape[0]
    bias_row = jnp.pad(gate_bias, (0, LANES - ngate)).reshape(1, LANES)
    return pl.pallas_call(
        _even_in_kernel,
        grid=(n // tm,),
        in_specs=[pl.BlockSpec((tm, d), lambda i: (i, 0)), _resident((1, d)), _resident((d, nout)),
                  _resident((ngate, d)), _resident((1, LANES)), _resident((ngate, 1))],
        out_specs=[pl.BlockSpec((tm, nout), lambda i: (i, 0)), pl.BlockSpec((ngate, tm), lambda i: (0, i))],
        out_shape=[jax.ShapeDtypeStruct((n, nout), F32), jax.ShapeDtypeStruct((ngate, n), F32)],
        compiler_params=_params("parallel"),
        name="even_in_proj",
    )(x2d, g.reshape(1, d), w_packed, w_gates_t, bias_row, gate_bias.reshape(ngate, 1))


def _extract_col(x, lane_idx):
    lane = lax.broadcasted_iota(jnp.int32, x.shape, 1)
    return jnp.sum(jnp.where(lane == lane_idx, x, 0.0), axis=1, keepdims=True)


def _mlstm_kernel(q_ref, k_ref, v_ref, o_ref, gate_ref, gt_ref, conv_ref, norm_ref, out_ref,
                  qc_ref, kc_ref, hf_ref, hb_ref, c_ref, m_ref, *, seq, chunk):
    L = chunk
    nc = seq // L
    d = M_HEAD_DIM
    head = pl.program_id(1)

    row_id = lax.broadcasted_iota(jnp.int32, (L, d), 0)

    def conv_silu(src_ref, w, r0, c):
        x = src_ref[0, pl.ds(r0, L), :]
        prev_row = src_ref[0, pl.ds(jnp.maximum(r0 - 1, 0), 1), :] * jnp.where(c > 0, 1.0, 0.0)
        next_row = src_ref[0, pl.ds(jnp.minimum(r0 + L, seq - 1), 1), :] * jnp.where(c < nc - 1, 1.0, 0.0)
        x_prev = jnp.where(row_id == 0, prev_row, pltpu.roll(x, 1, axis=0))
        x_next = jnp.where(row_id == L - 1, next_row, pltpu.roll(x, L - 1, axis=0))
        return _silu(x_prev * w[0:1, :] + x * w[1:2, :] + x_next * w[2:3, :])

    def prologue(c, carry):
        r0 = pl.multiple_of(c * L, L)
        qc_ref[pl.ds(r0, L), :] = (conv_silu(q_ref, conv_ref[0], r0, c) * (d ** -0.5)).astype(qc_ref.dtype)
        kc_ref[pl.ds(r0, L), :] = conv_silu(k_ref, conv_ref[1], r0, c)
        return carry

    lax.fori_loop(0, nc, prologue, 0)

    c_ref[...] = jnp.zeros_like(c_ref)
    m_ref[...] = jnp.zeros_like(m_ref)
    ones = jnp.ones((L, d), MXU_DTYPE)

    masks = [_scan_masks(L, reverse)[0] for reverse in (False, True)]
    dst_refs = (hf_ref, hb_ref)
    cb = MLSTM_GROUP_CHUNKS
    sub_id = lax.broadcasted_iota(jnp.int32, (SUBLANES, L), 0)

    def row_cumsum(row, mask):
        hi = _mx(row).astype(F32)
        mid = _mx(row - hi).astype(F32)
        lo = row - hi - mid
        terms = jnp.where(sub_id == 0, hi, jnp.where(sub_id == 1, mid, jnp.where(sub_id == 2, lo, 0.0)))
        return jnp.sum(_dot_nt(terms, mask), axis=0, keepdims=True)

    def body(i, carry):
        chains = [(dr, j) for j in range(cb) for dr in range(2)]
        chunk_of = lambda dr, j: (i * cb + j) if dr == 0 else (nc - 1 - (i * cb + j))
        rows = [pl.ds(pl.multiple_of(chunk_of(dr, j) * L, L), L) for dr, j in chains]
        each = lambda f, *lists: [f(*args) for args in zip(*lists)]

        g = each(lambda r_: gate_ref[0, r_, :], rows)
        bcum = each(lambda dj, g_: _scan_cumsum(masks[dj[0]], g_, one_matmul=False), chains, g)
        ig_col = each(lambda dj, g_: _extract_col(g_, _GATE_IG + dj[0] * M_HEADS + head), chains, g)
        b_col = each(lambda dj, b_: _extract_col(b_, _GATE_FG + dj[0] * M_HEADS + head), chains, bcum)
        g_tot = each(lambda dj, b_: b_[0:1, :] if dj[0] == 1 else b_[L - 1:L, :], chains, b_col)
        ig_row = each(lambda dj, r_: gt_ref[pl.ds(_GATE_IG + dj[0] * M_HEADS + head, 1), r_], chains, rows)
        fg_row = each(lambda dj, r_: gt_ref[pl.ds(_GATE_FG + dj[0] * M_HEADS + head, 1), r_], chains, rows)
        b_row = each(lambda dj, f_: row_cumsum(f_, _mx(masks[dj[0]])), chains, fg_row)
        dmat = each(lambda dj, bc_, br_, ir_: jnp.where(masks[dj[0]], bc_ - br_ + ir_, NEG_BIG),
                    chains, b_col, b_row, ig_row)
        dmax = each(lambda d_: jnp.max(d_, axis=1, keepdims=True), dmat)
        q = each(lambda r_: qc_ref[r_, :], rows)
        k = each(lambda r_: kc_ref[r_, :], rows)
        v_aug = each(lambda r_: jnp.concatenate([_mx(v_ref[0, r_, :]), ones], axis=1), rows)
        sc = each(lambda q_, k_, d_, m_: _mx(_dot_nt(q_, k_) * jnp.exp(d_ - m_)), q, k, dmat, dmax)
        nd_intra = each(_dot, sc, v_aug)
        kw_log = each(lambda g_, b_, i_: g_ - b_ + i_, g_tot, b_col, ig_col)
        kw_max = each(lambda w_: jnp.max(w_, axis=0, keepdims=True), kw_log)
        d_c = each(lambda k_, w_, a_, v_: _dot_tn(_mx(k_ * jnp.exp(w_ - a_)), v_), k, kw_log, kw_max, v_aug)

        c_start = [None] * len(chains)
        m_start = [None] * len(chains)
        for dr in range(2):
            c_cur = c_ref[dr]
            m_cur = m_ref[dr][:, 0:1]
            for j in range(cb):
                idx = chains.index((dr, j))
                c_start[idx], m_start[idx] = _mx(c_cur), m_cur
                m_new = jnp.maximum(g_tot[idx] + m_cur, kw_max[idx])
                c_cur = jnp.exp(g_tot[idx] + m_cur - m_new) * c_cur + jnp.exp(kw_max[idx] - m_new) * d_c[idx]
                m_cur = m_new
            c_ref[dr] = c_cur
            m_ref[dr] = jnp.broadcast_to(m_cur, (1, LANES))

        nd_inter = each(_dot, q, c_start)
        for idx, (dr, j) in enumerate(chains):
            inter_log = b_col[idx] + m_start[idx]
            m_t = jnp.maximum(inter_log, dmax[idx])
            numden = jnp.exp(dmax[idx] - m_t) * nd_intra[idx] + jnp.exp(inter_log - m_t) * nd_inter[idx]
            dst_refs[dr][rows[idx], :] = numden[:, :d] / jnp.maximum(jnp.abs(numden[:, d:]), jnp.exp(-m_t))
        return carry

    lax.fori_loop(0, nc // cb, body, 0)

    def epilogue(c, carry):
        r0 = pl.multiple_of(c * L, L)
        h = hf_ref[pl.ds(r0, L), :] + hb_ref[pl.ds(r0, L), :]
        h = _rms(h, norm_ref[...])
        out_ref[0, pl.ds(r0, L), :] = h * jax.nn.sigmoid(o_ref[0, pl.ds(r0, L), :])
        return carry

    lax.fori_loop(0, nc, epilogue, 0)


def mlstm_mixer(p, gates_t, conv_qk, m_norm, *, chunk):
    b, t, _ = p.shape
    d = M_HEAD_DIM
    ngate = gates_t.shape[0]
    blk = lambda off: pl.BlockSpec((1, t, d), lambda i, h, off=off: (i, 0, off + h))
    conv = conv_qk.reshape(CONV_WIDTH, 2, M_HEADS, d).transpose(2, 1, 0, 3)
    return pl.pallas_call(
        functools.partial(_mlstm_kernel, seq=t, chunk=chunk),
        grid=(b, M_HEADS),
        in_specs=[blk(_EV_MQ), blk(_EV_MK), blk(_EV_MV), blk(_EV_MO),
                  pl.BlockSpec((1, t, LANES), lambda i, h: (i, 0, _EV_GATE)),
                  pl.BlockSpec((ngate, t), lambda i, h: (0, i)),
                  pl.BlockSpec((None, 2, CONV_WIDTH, d), lambda i, h: (h, 0, 0, 0)),
                  pl.BlockSpec((1, d), lambda i, h: (0, h))],
        out_specs=pl.BlockSpec((1, t, d), lambda i, h: (i, 0, h)),
        out_shape=jax.ShapeDtypeStruct((b, t, M_HEADS * d), F32),
        scratch_shapes=[pltpu.VMEM((t, d), MXU_DTYPE),
                        pltpu.VMEM((t, d), F32),
                        pltpu.VMEM((t, d), F32),
                        pltpu.VMEM((t, d), F32),
                        pltpu.VMEM((2, d, 2 * d), F32),
                        pltpu.VMEM((2, 1, LANES), F32)],
        compiler_params=_params("parallel", "parallel"),
        name="mlstm",
    )(p, p, p, p, p, gates_t, conv, m_norm.reshape(1, M_HEADS * d))


def _gla_kernel(q_ref, k_ref, v0_ref, v1_ref, g0_ref, g1_ref, gate_ref, w2_ref, b2_ref, norm_ref, out_ref,
                of_ref, ob_ref, s_ref, *, seq, chunk):
    L = chunk
    nc = seq // L
    dk2 = 2 * G_KEY_DIM
    dv = G_VAL_DIM
    lane_k = lax.broadcasted_iota(jnp.int32, (L, dk2), 1)
    head0 = (lane_k < G_KEY_DIM).astype(F32)
    head1 = 1.0 - head0
    s_row = lax.broadcasted_iota(jnp.int32, (2 * dv, dk2), 0)
    s_col = lax.broadcasted_iota(jnp.int32, (2 * dv, dk2), 1)
    s_mask = ((s_row < dv) == (s_col < G_KEY_DIM)).astype(F32)

    s_ref[...] = jnp.zeros_like(s_ref)
    masks = [_scan_masks(L, reverse)[0] for reverse in (False, True)]
    masks2 = [jnp.concatenate([m, m], axis=0) for m in masks]
    dst_refs = (of_ref, ob_ref)
    cb = GLA_GROUP_CHUNKS

    def body(i, carry):
        chains = [(d, j) for j in range(cb) for d in range(2)]
        chunk_of = lambda d, j: (i * cb + j) if d == 0 else (nc - 1 - (i * cb + j))
        rows = [pl.ds(pl.multiple_of(chunk_of(d, j) * L, L), L) for d, j in chains]
        each = lambda f, *lists: [f(*args) for args in zip(*lists)]

        z = each(lambda dj, r_: _dot(gate_ref[0, r_, :], w2_ref[dj[0], 0]) + b2_ref[dj[0], 0], chains, rows)
        la = each(lambda z_: _log_sigmoid(z_) * (1.0 / G_DECAY_TAU), z)
        bcum = each(lambda dj, l_: _scan_cumsum(masks[dj[0]], l_), chains, la)
        g_tot = each(lambda dj, b_: b_[0:1, :] if dj[0] == 1 else b_[L - 1:L, :], chains, bcum)
        mid = each(lambda b_: b_[L // 2:L // 2 + 1, :], bcum)
        q = each(lambda r_: q_ref[0, r_, :] * (G_KEY_DIM ** -0.5), rows)
        k = each(lambda r_: k_ref[0, r_, :], rows)
        v = each(lambda r_: jnp.concatenate([_mx(v0_ref[0, r_, :]), _mx(v1_ref[0, r_, :])], axis=1), rows)
        q_in = each(lambda q_, b_, m_: q_ * jnp.exp(b_ - m_), q, bcum, mid)
        q_in2 = each(lambda q_: _mx(jnp.concatenate([q_ * head0, q_ * head1], axis=0)), q_in)
        k_in = each(lambda k_, b_, m_: _mx(k_ * jnp.exp(m_ - b_)), k, bcum, mid)
        scores = each(lambda dj, q_, k_: _mx(jnp.where(masks2[dj[0]], _dot_nt(q_, k_), 0.0)), chains, q_in2, k_in)
        o_intra = each(lambda a_, v_: jnp.concatenate([_dot(a_[:L], v_[:, :dv]), _dot(a_[L:], v_[:, dv:])], axis=1),
                       scores, v)
        q_hat = each(lambda q_, b_: _mx(q_ * jnp.exp(b_)), q, bcum)
        k_hat = each(lambda k_, b_, g_: _mx(k_ * jnp.exp(g_ - b_)), k, bcum, g_tot)
        d_s = each(lambda v_, k_: _dot_tn(v_, k_) * s_mask, v, k_hat)
        decay = each(jnp.exp, g_tot)

        s_at_start = [None] * len(chains)
        for d in range(2):
            s_cur = s_ref[d]
            for j in range(cb):
                idx = chains.index((d, j))
                s_at_start[idx] = _mx(s_cur)
                s_cur = s_cur * decay[idx] + d_s[idx]
            s_ref[d] = s_cur
        o_inter = each(_dot_nt, q_hat, s_at_start)
        for idx, (d, j) in enumerate(chains):
            dst_refs[d][rows[idx], :] = o_intra[idx] + o_inter[idx]
        return carry

    lax.fori_loop(0, nc // cb, body, 0)

    rows_out = L * cb

    def epilogue(c, carry):
        rows = pl.ds(pl.multiple_of(c * rows_out, rows_out), rows_out)
        o = of_ref[rows, :] + ob_ref[rows, :]
        g = jnp.concatenate([g0_ref[0, rows, :], g1_ref[0, rows, :]], axis=1)
        nrm = norm_ref[...]
        o = jnp.concatenate([_rms(o[:, :dv], nrm[:, :dv]), _rms(o[:, dv:], nrm[:, dv:])], axis=1)
        out_ref[0, rows, :] = o * _silu(g)
        return carry

    lax.fori_loop(0, nc // cb, epilogue, 0)


def gla_mixer(p, decay_w2, decay_b, g_norm, *, chunk):
    b, t, _ = p.shape
    dv = G_VAL_DIM
    pairs = G_HEADS // 2
    dk2 = 2 * G_KEY_DIM
    w2 = jnp.zeros((2, LANES, pairs * dk2), F32)
    for direction in range(2):
        lo = _GATE_LR + direction * G_DECAY_RANK
        w2 = w2.at[direction, lo:lo + G_DECAY_RANK].set(decay_w2[direction])
    w2 = _mx(w2.reshape(2, LANES, pairs, dk2).transpose(0, 2, 1, 3))
    b2 = decay_b.reshape(2, pairs, 1, dk2)
    blk = lambda off, mult: pl.BlockSpec((1, t, LANES), lambda i, h, off=off, mult=mult: (i, 0, off + mult * h))
    return pl.pallas_call(
        functools.partial(_gla_kernel, seq=t, chunk=chunk),
        grid=(b, pairs),
        in_specs=[blk(_EV_GQ, 1), blk(_EV_GK, 1), blk(_EV_GV, 2), blk(_EV_GV + 1, 2),
                  blk(_EV_GG, 2), blk(_EV_GG + 1, 2),
                  pl.BlockSpec((1, t, LANES), lambda i, h: (i, 0, _EV_GATE)),
                  pl.BlockSpec((2, 1, LANES, dk2), lambda i, h: (0, h, 0, 0)),
                  pl.BlockSpec((2, 1, 1, dk2), lambda i, h: (0, h, 0, 0)),
                  pl.BlockSpec((1, 2 * dv), lambda i, h: (0, h))],
        out_specs=pl.BlockSpec((1, t, 2 * dv), lambda i, h: (i, 0, h)),
        out_shape=jax.ShapeDtypeStruct((b, t, G_HEADS * dv), F32),
        scratch_shapes=[pltpu.VMEM((t, 2 * dv), F32),
                        pltpu.VMEM((t, 2 * dv), F32),
                        pltpu.VMEM((2, 2 * dv, dk2), F32)],
        compiler_params=_params("parallel", "parallel"),
        name="gla",
    )(p, p, p, p, p, p, p, w2, b2, g_norm.reshape(1, G_HEADS * dv))


def _rwkv_in_kernel(x_ref, xp_ref, xn_ref, nrm_ref, mu_ref, wrkv_ref, w1_ref, w2_ref, w0_ref,
                    a1_ref, a2_ref, a0_ref, g1_ref, g2_ref,
                    r_ref, k_ref, v_ref, a_ref, lwf_ref, lwb_ref, g_ref, *, tm):
    i = pl.program_id(1)
    nt = pl.num_programs(1)
    nrm = nrm_ref[...]
    h = _rms(x_ref[0], nrm)
    before = _rms(xp_ref[0, 0], nrm)[SUBLANES - 1:SUBLANES, :] * jnp.where(i > 0, 1.0, 0.0)
    after = _rms(xn_ref[0, 0], nrm)[0:1, :] * jnp.where(i < nt - 1, 1.0, 0.0)
    row = lax.broadcasted_iota(jnp.int32, h.shape, 0)
    h_prev = jnp.where(row == 0, before, pltpu.roll(h, 1, axis=0))
    h_next = jnp.where(row == tm - 1, after, pltpu.roll(h, tm - 1, axis=0))
    hh = 0.5 * (h_prev + h_next) - h
    mix = lambda j: h + hh * mu_ref[j:j + 1, :]
    xr, xw, xk, xv, xa, xg = (mix(j) for j in range(6))
    r_ref[0] = _dot(xr, wrkv_ref[0]).astype(r_ref.dtype)
    k_ref[0] = _dot(xk, wrkv_ref[1]).astype(k_ref.dtype)
    v_ref[0] = _dot(xv, wrkv_ref[2]).astype(v_ref.dtype)
    a_ref[0] = jax.nn.sigmoid(a0_ref[...] + _dot(_dot(xa, a1_ref[...]), a2_ref[...])).astype(a_ref.dtype)
    g_ref[0] = _dot(jax.nn.sigmoid(_dot(xg, g1_ref[...])), g2_ref[...]).astype(g_ref.dtype)
    for direction, dst in ((0, lwf_ref), (1, lwb_ref)):
        z = w0_ref[direction:direction + 1, :] + _dot(jnp.tanh(_dot(xw, w1_ref[direction])), w2_ref[direction])
        dst[0] = jax.nn.sigmoid(z) * -math.exp(-0.5)


def rwkv_in(x, norm, mu, w_rkv, w0, w1, w2, a0, a1, a2, g1, g2, *, tm):
    b, t, d = x.shape
    x8 = x.reshape(b, t // SUBLANES, SUBLANES, d)
    tb = tm // SUBLANES
    nb = t // SUBLANES
    rank_g = g1.shape[1]
    pad_g = -rank_g % LANES
    g1p = jnp.pad(g1, ((0, 0), (0, pad_g)))
    g2p = jnp.pad(g2, ((0, pad_g), (0, 0)))
    tile = pl.BlockSpec((1, tm, d), lambda i, j: (i, j, 0))
    out = [jax.ShapeDtypeStruct((b, t, d), dt) for dt in (MXU_DTYPE,) * 4 + (F32, F32, MXU_DTYPE)]
    weights = [norm.reshape(1, d), mu, _mx(w_rkv), _mx(w1), _mx(w2), w0,
               _mx(a1), _mx(a2), a0.reshape(1, d), _mx(g1p), _mx(g2p)]
    return pl.pallas_call(
        functools.partial(_rwkv_in_kernel, tm=tm),
        grid=(b, t // tm),
        in_specs=[tile,
                  pl.BlockSpec((1, 1, SUBLANES, d), lambda i, j: (i, jnp.maximum(j * tb - 1, 0), 0, 0)),
                  pl.BlockSpec((1, 1, SUBLANES, d), lambda i, j: (i, jnp.minimum((j + 1) * tb, nb - 1), 0, 0))]
                 + [_resident(w.shape) for w in weights],
        out_specs=[tile] * 7,
        out_shape=out,
        compiler_params=_params("parallel", "parallel"),
        name="rwkv_in",
    )(x, x8, x8, *weights)


def _rwkv_scan_kernel(r_ref, k_ref, v_ref, a_ref, lwf_ref, lwb_ref, kk_ref, ka_ref, rk_ref, lnw_ref, lnb_ref,
                      out_ref, y_ref, rq_ref, m_ref, sadd_ref, gam_ref, s_ref, *, seq, chunk):
    L = chunk
    nc = seq // L
    n = R_HEAD_DIM
    lane = lax.broadcasted_iota(jnp.int32, (L, 2 * n), 1)
    head0 = (lane < n).astype(F32)
    head1 = 1.0 - head0

    def head_sum(x):
        s0 = jnp.sum(x * head0, axis=1, keepdims=True)
        s1 = jnp.sum(x * head1, axis=1, keepdims=True)
        return s0 * head0 + s1 * head1

    head0_mx, head1_mx = _mx(head0), _mx(head1)

    def stack(x):
        x = _mx(x)
        return jnp.concatenate([x * head0_mx, x * head1_mx], axis=0)

    def load(c):
        r0 = pl.multiple_of(c * L, L)
        rows = pl.ds(r0, L)
        r, k, v, a = (ref[0, rows, :].astype(F32) for ref in (r_ref, k_ref, v_ref, a_ref))
        kk = k * kk_ref[...]
        kk = kk / jnp.maximum(jnp.sqrt(head_sum(kk * kk)), 1e-12)
        kp = k * (1.0 + (a - 1.0) * ka_ref[...])
        return rows, r, kp, v, -kk, kk * a

    lane2 = lax.broadcasted_iota(jnp.int32, (L, 4 * n), 1)
    head0_2 = ((lane2 & (2 * n - 1)) < n).astype(F32)
    head0_2_mx, head1_2_mx = _mx(head0_2), _mx(1.0 - head0_2)

    def stack2(x):
        x = _mx(x)
        return jnp.concatenate([x * head0_2_mx, x * head1_2_mx], axis=0)

    masks_l = [_scan_masks(L, reverse)[0] for reverse in (False, True)]
    row_cat = lax.broadcasted_iota(jnp.int32, (L, 2 * L), 0)
    col_cat = lax.broadcasted_iota(jnp.int32, (L, 2 * L), 1) & (L - 1)
    masks = [(col_cat <= row_cat, col_cat < row_cat), (col_cat >= row_cat, col_cat > row_cat)]
    eye_cat = (col_cat == row_cat).astype(F32)
    same_head = ((lax.broadcasted_iota(jnp.int32, (2 * n, 2 * n), 0) < n)
                 == (lax.broadcasted_iota(jnp.int32, (2 * n, 2 * n), 1) < n)).astype(F32)
    lw_refs = (lwf_ref, lwb_ref)
    cb = min(RWKV_BUILD_CHUNKS, nc // 2)
    groups = nc // cb
    chunk_of = lambda d, g, j: (g * cb + j) if d == 0 else (nc - 1 - (g * cb + j))
    each = lambda f, *lists: [f(*args) for args in zip(*lists)]

    def finish(c, y):
        rows = pl.ds(pl.multiple_of(c * L, L), L)
        r, k, v, a = (ref[0, rows, :].astype(F32) for ref in (r_ref, k_ref, v_ref, a_ref))
        kp = k * (1.0 + (a - 1.0) * ka_ref[...])
        mean = head_sum(y) * (1.0 / n)
        yc = y - mean
        var = head_sum(yc * yc) * (1.0 / n)
        yn = yc * lax.rsqrt(var + R_GN_EPS) * lnw_ref[...] + lnb_ref[...]
        out_ref[0, rows, :] = yn + head_sum(r * kp * rk_ref[...]) * v

    def sweep_steps(g, crossed):
        state = [s_ref[d] for d in range(2)]

        def step(j):
            cs = [chunk_of(d, g, j) for d in range(2)]
            rows = [pl.ds(pl.multiple_of(c * L, L), L) for c in cs]
            s_mx = each(_mx, state)
            y = [y_ref[d, rows[d], :] + _dot_nt(rq_ref[d, rows[d], :], s_mx[d]) for d in range(2)]
            for d in range(2):
                state[d] = (state[d] * gam_ref[d, pl.ds(cs[d], 1), :] + _dot(s_mx[d], m_ref[d, cs[d]])
                            + sadd_ref[d, cs[d]])
            if j == cb - 1:
                for d in range(2):
                    s_ref[d] = state[d]
            for d in range(2):
                if crossed:
                    finish(cs[d], y[d] + y_ref[1 - d, rows[d], :])
                else:
                    y_ref[d, rows[d], :] = y[d]

        return [functools.partial(step, j) for j in range(cb)]

    def build_group(g, interleaved=()):
        pending = list(interleaved)
        run_one = lambda: pending.pop(0)() if pending else None
        chains = [(d, j) for j in range(cb) for d in range(2)]
        chunk_ids = [chunk_of(d, g, j) for d, j in chains]
        loaded = [load(c) for c in chunk_ids]

        lw = each(lambda dj, ld: lw_refs[dj[0]][0, ld[0], :], chains, loaded)
        cin = each(lambda dj, x: _scan_cumsum(masks_l[dj[0]], x), chains, lw)
        ctot = each(lambda dj, c_: c_[0:1, :] if dj[0] == 1 else c_[L - 1:L, :], chains, cin)
        e_out = each(lambda c_: jnp.exp(-c_), cin)
        e_end = each(lambda c_, t_: jnp.exp(t_ - c_), cin, ctot)
        a_hat = each(lambda ld, c_, l_: ld[4] * jnp.exp(c_ - l_), loaded, cin, lw)
        r_hat = each(lambda ld, c_: ld[1] * jnp.exp(c_), loaded, cin)
        x = each(lambda a_, r_: _mx(jnp.concatenate([a_, r_], axis=0)), a_hat, r_hat)
        yk = each(lambda ld, e_: jnp.concatenate([stack(ld[5] * e_), stack(ld[2] * e_)], axis=0), loaded, e_out)
        run_one()
        gram = each(_dot_nt, x, yk)
        a_ab = each(lambda dj, g_: jnp.where(masks[dj[0]][1], g_[:L, :2 * L], 0.0), chains, gram)
        a_ak = each(lambda dj, g_: _mx(jnp.where(masks[dj[0]][1], g_[:L, 2 * L:], 0.0)), chains, gram)
        a_rb = each(lambda dj, g_: _mx(jnp.where(masks[dj[0]][0], g_[L:, :2 * L], 0.0)), chains, gram)
        a_rk = each(lambda dj, g_: _mx(jnp.where(masks[dj[0]][0], g_[L:, 2 * L:], 0.0)), chains, gram)

        t_inv = each(lambda n_: eye_cat + n_, a_ab)
        power = each(lambda n_: _mx(_dot(n_, stack(n_))), a_ab)
        for _ in range(L.bit_length() - 3):
            both = each(lambda p_, t_: _dot(p_, stack2(jnp.concatenate([p_, _mx(t_)], axis=1))), power, t_inv)
            power = each(lambda b_: _mx(b_[:, :2 * L]), both)
            t_inv = each(lambda t_, b_: t_ + b_[:, 2 * L:], t_inv, both)
            run_one()
        t_inv = each(lambda p_, t_: t_ + _dot(p_, stack(t_)), power, t_inv)

        v2 = each(lambda ld: stack(ld[3]), loaded)
        av = each(_dot, a_ak, v2)
        w = each(lambda t_, a_, av_: _dot(t_, stack2(jnp.concatenate([_mx(a_), _mx(av_)], axis=1))),
                 t_inv, a_hat, av)
        z = each(lambda b_, w_: _dot(b_, stack2(w_)), a_rb, w)
        y0 = each(lambda z_, k_, v_: z_[:, 2 * n:] + _dot(k_, v_), z, a_rk, v2)
        while pending:
            run_one()
        b_end = each(lambda ld, e_: _mx(ld[5] * e_), loaded, e_end)
        k_end = each(lambda ld, e_: _mx(ld[2] * e_), loaded, e_end)
        m_lr = each(lambda w_, b_: _dot_tn(w_[:, :2 * n], b_) * same_head, w, b_end)
        s_add = each(lambda w_, ld, b_, k_: _dot_tn(jnp.concatenate([w_[:, 2 * n:], ld[3]], axis=0),
                                                    jnp.concatenate([b_, k_], axis=0)) * same_head,
                     w, loaded, b_end, k_end)
        for idx, (d, j) in enumerate(chains):
            rows = loaded[idx][0]
            y_ref[d, rows, :] = y0[idx]
            rq_ref[d, rows, :] = (r_hat[idx] + z[idx][:, :2 * n]).astype(rq_ref.dtype)
            m_ref[d, chunk_ids[idx]] = m_lr[idx].astype(m_ref.dtype)
            sadd_ref[d, chunk_ids[idx]] = s_add[idx]
            gam_ref[d, pl.ds(chunk_ids[idx], 1), :] = jnp.exp(ctot[idx])

    s_ref[...] = jnp.zeros_like(s_ref)
    build_group(0)

    def before_crossing(g, carry):
        build_group(g, sweep_steps(g - 1, crossed=False))
        return carry

    def after_crossing(g, carry):
        build_group(g, sweep_steps(g - 1, crossed=True))
        return carry

    lax.fori_loop(1, groups // 2 + 1, before_crossing, 0)
    lax.fori_loop(groups // 2 + 1, groups, after_crossing, 0)
    for step in sweep_steps(groups - 1, crossed=groups > 1):
        step()


def rwkv_scan(r, k, v, a, lwf, lwb, k_k, k_a, r_k, ln_w, ln_b, *, chunk):
    b, t, d = r.shape
    w = 2 * R_HEAD_DIM
    pairs = d // w
    nc = t // chunk
    assert chunk == R_HEAD_DIM and nc % 2 == 0, (chunk, t)
    blk = pl.BlockSpec((1, t, w), lambda i, h: (i, 0, h))
    vec = pl.BlockSpec((1, w), lambda i, h: (0, h))
    row = lambda z: z.reshape(1, d)
    return pl.pallas_call(
        functools.partial(_rwkv_scan_kernel, seq=t, chunk=chunk),
        grid=(b, pairs),
        in_specs=[blk] * 6 + [vec] * 5,
        out_specs=blk,
        out_shape=jax.ShapeDtypeStruct((b, t, d), F32),
        scratch_shapes=[pltpu.VMEM((2, t, w), F32),
                        pltpu.VMEM((2, t, w), MXU_DTYPE),
                        pltpu.VMEM((2, nc, w, w), MXU_DTYPE),
                        pltpu.VMEM((2, nc, w, w), F32),
                        pltpu.VMEM((2, nc, w), F32),
                        pltpu.VMEM((2, w, w), F32)],
        compiler_params=_params("parallel", "parallel"),
        name="rwkv_scan",
    )(r, k, v, a, lwf, lwb, row(k_k), row(k_a), row(r_k), row(ln_w), row(ln_b))


def _xattn_kernel(x_ref, a_ref, b_ref, wm_ref, nrm_ref, wq_ref, k_ref, v_ref, wo_ref, o_ref, *, gated):
    if gated:
        merged = a_ref[0] * b_ref[0].astype(F32)
    else:
        merged = jnp.concatenate([_mx(a_ref[0]), _mx(b_ref[0])], axis=1)
    x = x_ref[0] + _dot(merged, wm_ref[...])
    d = x.shape[-1]
    dh = d // X_HEADS
    q = _mx(_dot(_rms(x, nrm_ref[...]), wq_ref[...]))
    cols = [slice(hd * dh, (hd + 1) * dh) for hd in range(X_HEADS)]
    s = [_dot_nt(q[:, c], k_ref[0, :, c]) * (dh ** -0.5) for c in cols]
    p = [jnp.exp(s_ - jnp.max(s_, axis=-1, keepdims=True)) for s_ in s]
    p = [_mx(p_ / jnp.sum(p_, axis=-1, keepdims=True)) for p_ in p]
    heads = [_mx(_dot(p_, v_ref[0, :, c])) for p_, c in zip(p, cols)]
    o_ref[0] = x + _dot(jnp.concatenate(heads, axis=1), wo_ref[...])


def mixer_out_cross_attention(x, a, b_in, w_mix, kv, norm, wq, wo, *, tm, gated):
    b, t, d = x.shape
    m = kv.shape[1]
    tile = lambda width: pl.BlockSpec((1, tm, width), lambda i, j: (i, j, 0))
    return pl.pallas_call(
        functools.partial(_xattn_kernel, gated=gated),
        grid=(b, t // tm),
        in_specs=[tile(d), tile(a.shape[-1]), tile(b_in.shape[-1]), _resident(w_mix.shape),
                  _resident((1, d)), _resident((d, d)),
                  pl.BlockSpec((1, m, d), lambda i, j: (i, 0, 0)),
                  pl.BlockSpec((1, m, d), lambda i, j: (i, 0, 1)),
                  _resident((d, d))],
        out_specs=tile(d),
        out_shape=jax.ShapeDtypeStruct((b, t, d), F32),
        compiler_params=_params("parallel", "parallel"),
        name="mixer_out_cross_attention",
    )(x, a, b_in, _mx(w_mix), norm.reshape(1, d), _mx(wq), kv, kv, _mx(wo))


def _ffn_kernel(x_ref, nrm_ref, wg_ref, wu_ref, wd_ref, fin_ref, o_ref, *, final_norm):
    x = x_ref[...]
    h = _mx(_rms(x, nrm_ref[...]))
    act = _silu(_dot(h, wg_ref[...])) * _dot(h, wu_ref[...])
    y = x + _dot(act, wd_ref[...])
    o_ref[...] = _rms(y, fin_ref[...]) if final_norm else y


def swiglu_ffn(x2d, norm, w_gate, w_up, w_down, final_gain, *, tm, final_norm):
    n, d = x2d.shape
    dff = w_gate.shape[1]
    row = pl.BlockSpec((tm, d), lambda i: (i, 0))
    return pl.pallas_call(
        functools.partial(_ffn_kernel, final_norm=final_norm),
        grid=(n // tm,),
        in_specs=[row, _resident((1, d)), _resident((d, dff)), _resident((d, dff)), _resident((dff, d)),
                  _resident((1, d))],
        out_specs=row,
        out_shape=jax.ShapeDtypeStruct((n, d), F32),
        compiler_params=_params("parallel"),
        name="swiglu_ffn",
    )(x2d, norm.reshape(1, d), _mx(w_gate), _mx(w_up), _mx(w_down), final_gain.reshape(1, d))


def _pack_even_in(w_in):
    d = w_in.shape[0]
    mw, gkw, gvw = M_HEADS * M_HEAD_DIM, G_HEADS * G_KEY_DIM, G_HEADS * G_VAL_DIM
    widths = (mw, mw, mw, mw, 2 * M_HEADS, 2 * M_HEADS, gkw, gkw, gvw, gvw, 2 * G_DECAY_RANK)
    starts = [0]
    for wdt in widths:
        starts.append(starts[-1] + wdt)
    mq, mk, mv, mo, mi, mf, gq, gk, gv, gg, glr = (w_in[:, starts[j]:starts[j + 1]] for j in range(len(widths)))
    gate = jnp.concatenate([mi, mf, glr], axis=1)
    gate = jnp.pad(gate, ((0, 0), (0, LANES - gate.shape[1])))
    packed = jnp.concatenate([mq, mk, mv, mo, gq, gk, gv, gg, gate], axis=1)
    assert packed.shape == (d, _EV_BLOCKS * LANES)
    gates_t = jnp.concatenate([mi, mf], axis=1).T
    return _mx(packed), _mx(gates_t)


def _even_mixer_layer(x, norm, w_in, conv_qk, ig_bias, fg_bias, m_norm, decay_w2, decay_b, g_norm, *, tm):
    b, t, d = x.shape
    x2d = x.reshape(b * t, d)
    w_packed, w_gates_t = _pack_even_in(w_in)
    gate_bias = jnp.concatenate([ig_bias.reshape(-1), fg_bias.reshape(-1)])
    p, gates_t = even_in_proj(x2d, norm, w_packed, w_gates_t, gate_bias, tm=tm)
    p = p.reshape(b, t, -1)
    hm = mlstm_mixer(p, gates_t, conv_qk, m_norm, chunk=min(MLSTM_CHUNK, t))
    og = gla_mixer(p, decay_w2, decay_b, g_norm, chunk=min(GLA_CHUNK, t))
    return hm, og


def _rwkv_mixer_layer(x, norm, mu, w_rkv, w0, w1, w2, a0, a1, a2, g1, g2, k_k, k_a, r_k, ln_w, ln_b, *, tm):
    b, t, d = x.shape
    r, k, v, a, lwf, lwb, g = rwkv_in(x, norm, mu, w_rkv, w0, w1, w2, a0, a1, a2, g1, g2, tm=tm)
    y = rwkv_scan(r, k, v, a, lwf, lwb, k_k, k_a, r_k.reshape(-1), ln_w, ln_b, chunk=min(RWKV_CHUNK, t))
    return y, g


def kernel(x, mem, norm_mix, norm_xattn, norm_mem, norm_ffn, norm_final, xa_wq, xa_wkv, xa_wo, ffn_w_gate, ffn_w_up, ffn_w_down, ev_w_in, ev_conv_qk, ev_m_ig_bias, ev_m_fg_bias, ev_m_norm, ev_g_decay_w2, ev_g_decay_b, ev_g_norm, ev_w_out, od_mu, od_w_rkv, od_w0, od_w1, od_w2, od_a0, od_a1, od_a2, od_g1, od_g2, od_k_k, od_k_a, od_r_k, od_ln_w, od_ln_b, od_w_o):
    b, t, d = x.shape
    m = mem.shape[1]
    depth = norm_mix.shape[0]
    tm = min(512, t)
    for layer in range(depth):
        if layer % 2 == 0:
            e = layer // 2
            mix_a, mix_b = _even_mixer_layer(x, norm_mix[layer], ev_w_in[e], ev_conv_qk[e], ev_m_ig_bias[e],
                                             ev_m_fg_bias[e], ev_m_norm[e], ev_g_decay_w2[e], ev_g_decay_b[e],
                                             ev_g_norm[e], tm=tm)
            w_mix = ev_w_out[e]
        else:
            o = layer // 2
            mix_a, mix_b = _rwkv_mixer_layer(x, norm_mix[layer], od_mu[o], od_w_rkv[o], od_w0[o], od_w1[o], od_w2[o],
                                             od_a0[o], od_a1[o], od_a2[o], od_g1[o], od_g2[o], od_k_k[o], od_k_a[o],
                                             od_r_k[o], od_ln_w[o], od_ln_b[o], tm=tm)
            w_mix = od_w_o[o]
        kv = rms_matmul(mem.reshape(b * m, d), norm_mem[layer], _mx(xa_wkv[layer]), tm=min(512, b * m),
                        out_dtype=MXU_DTYPE).reshape(b, m, 2 * d)
        x = mixer_out_cross_attention(x, mix_a, mix_b, w_mix, kv, norm_xattn[layer], xa_wq[layer], xa_wo[layer],
                                      tm=tm, gated=layer % 2 == 1)
        last = layer == depth - 1
        x = swiglu_ffn(x.reshape(b * t, d), norm_ffn[layer], ffn_w_gate[layer], ffn_w_up[layer], ffn_w_down[layer],
                       norm_final, tm=tm, final_norm=last).reshape(b, t, d)
    return x
```

```python
import functools
import math

import jax
import jax.numpy as jnp
from jax import lax
from jax.experimental import pallas as pl
from jax.experimental.pallas import tpu as pltpu

F32 = jnp.float32
MXU_DTYPE = jnp.bfloat16
LANES = 128
SUBLANES = 8
VMEM_LIMIT_BYTES = 60 * 1024 * 1024
NEG_BIG = -1e30

EPS = 1e-6
CONV_WIDTH = 3
M_HEADS = 4
M_HEAD_DIM = 128
G_HEADS = 4
G_KEY_DIM = 64
G_VAL_DIM = 128
G_DECAY_RANK = 16
G_DECAY_TAU = 16.0
R_HEAD_DIM = 64
R_GN_EPS = 64e-5
R_KK_NORM_FLOOR = 1e-12
X_HEADS = 4

TOKEN_TILE = 512
XATTN_TOKEN_TILE = 1024
MLSTM_CHUNK = 256
GLA_CHUNK = 64
RWKV_CHUNK = 64
MLSTM_GROUP_CHUNKS = 2
GLA_GROUP_CHUNKS = 8
RWKV_BUILD_CHUNKS = 8

_EV_MQ, _EV_MK, _EV_MV, _EV_MO = 0, 4, 8, 12
_EV_GQ, _EV_GK, _EV_GV, _EV_GG, _EV_GATE = 16, 18, 20, 24, 28
_EV_BLOCKS = 29
_GATE_IG, _GATE_FG, _GATE_LR = 0, 8, 16


def _mx(a):
    return a.astype(MXU_DTYPE)


def _dot(a, b):
    return jnp.dot(_mx(a), _mx(b), preferred_element_type=F32)


def _dot_nt(a, b):
    return lax.dot_general(_mx(a), _mx(b), (((1,), (1,)), ((), ())), preferred_element_type=F32)


def _dot_tn(a, b):
    return lax.dot_general(_mx(a), _mx(b), (((0,), (0,)), ((), ())), preferred_element_type=F32)


def _scan_cumsum(mask, x, short_chunk=True):
    m = _mx(mask)
    hi = _mx(x)
    rest = x - hi.astype(F32)
    mid = _mx(rest)
    if short_chunk:
        width = x.shape[1]
        sums = jnp.dot(m, jnp.concatenate([hi, mid], axis=1), preferred_element_type=F32)
        return sums[:, :width] + sums[:, width:]
    lo = _mx(rest - mid.astype(F32))
    dot = lambda t: jnp.dot(m, t, preferred_element_type=F32)
    return dot(hi) + dot(mid) + dot(lo)


def _rms(x, g):
    return x * lax.rsqrt(jnp.mean(x * x, axis=-1, keepdims=True) + EPS) * g


def _log_sigmoid(x):
    return jnp.minimum(x, 0.0) - jnp.log1p(jnp.exp(-jnp.abs(x)))


def _silu(x):
    return x * jax.nn.sigmoid(x)


def _params(*sem):
    return pltpu.CompilerParams(dimension_semantics=sem, vmem_limit_bytes=VMEM_LIMIT_BYTES)


def _resident(shape):
    nd = len(shape)
    return pl.BlockSpec(shape, lambda *_: (0,) * nd, pipeline_mode=pl.Buffered(1))


def _scan_mask(n, reverse):
    row = lax.broadcasted_iota(jnp.int32, (n, n), 0)
    col = lax.broadcasted_iota(jnp.int32, (n, n), 1)
    return col >= row if reverse else col <= row


def _rms_matmul_kernel(x_ref, g_ref, w_ref, o_ref):
    h = _rms(x_ref[...], g_ref[...])
    o_ref[...] = _dot(h, w_ref[...]).astype(o_ref.dtype)


def rms_matmul(x2d, g, w, *, tm, out_dtype):
    n, d = x2d.shape
    nout = w.shape[1]
    return pl.pallas_call(
        _rms_matmul_kernel,
        grid=(n // tm,),
        in_specs=[pl.BlockSpec((tm, d), lambda i: (i, 0)), _resident((1, d)), _resident((d, nout))],
        out_specs=pl.BlockSpec((tm, nout), lambda i: (i, 0)),
        out_shape=jax.ShapeDtypeStruct((n, nout), out_dtype),
        compiler_params=_params("parallel"),
        name="rms_matmul",
    )(x2d, g.reshape(1, d), w)


def _even_in_kernel(x_ref, g_ref, w_ref, wgt_ref, brow_ref, bcol_ref, o_ref, gt_ref):
    h = _mx(_rms(x_ref[...], g_ref[...]))
    p = _dot(h, w_ref[...])
    main = (_EV_BLOCKS - 1) * LANES
    o_ref[:, :main] = p[:, :main]
    gates = p[:, main:] + brow_ref[...]
    lane = lax.broadcasted_iota(jnp.int32, gates.shape, 1)
    o_ref[:, main:] = jnp.where((lane >= _GATE_FG) & (lane < _GATE_LR), _log_sigmoid(gates), gates)
    gates_t = _dot_nt(wgt_ref[...], h) + bcol_ref[...]
    row = lax.broadcasted_iota(jnp.int32, gates_t.shape, 0)
    gt_ref[...] = jnp.where(row >= _GATE_FG, _log_sigmoid(gates_t), gates_t)


def even_in_proj(x2d, g, w_packed, w_gates_t, gate_bias, *, tm):
    n, d = x2d.shape
    nout = w_packed.shape[1]
    ngate = w_gates_t.shape[0]
    bias_row = jnp.pad(gate_bias, (0, LANES - ngate)).reshape(1, LANES)
    return pl.pallas_call(
        _even_in_kernel,
        grid=(n // tm,),
        in_specs=[pl.BlockSpec((tm, d), lambda i: (i, 0)), _resident((1, d)), _resident((d, nout)),
                  _resident((ngate, d)), _resident((1, LANES)), _resident((ngate, 1))],
        out_specs=[pl.BlockSpec((tm, nout), lambda i: (i, 0)), pl.BlockSpec((ngate, tm), lambda i: (0, i))],
        out_shape=[jax.ShapeDtypeStruct((n, nout), F32), jax.ShapeDtypeStruct((ngate, n), F32)],
        compiler_params=_params("parallel"),
        name="even_in_proj",
    )(x2d, g.reshape(1, d), w_packed, w_gates_t, bias_row, gate_bias.reshape(ngate, 1))


def _extract_col(x, lane_idx):
    lane = lax.broadcasted_iota(jnp.int32, x.shape, 1)
    return jnp.sum(jnp.where(lane == lane_idx, x, 0.0), axis=1, keepdims=True)


def _mlstm_kernel(q_ref, k_ref, v_ref, o_ref, gate_ref, gt_ref, conv_ref, norm_ref, out_ref,
                  qc_ref, kc_ref, hf_ref, hb_ref, c_ref, m_ref, *, seq, chunk):
    L = chunk
    nc = seq // L
    d = M_HEAD_DIM
    head = pl.program_id(1)

    row_id = lax.broadcasted_iota(jnp.int32, (L, d), 0)

    def conv_silu(src_ref, w, r0, c):
        x = src_ref[0, pl.ds(r0, L), :]
        prev_row = src_ref[0, pl.ds(jnp.maximum(r0 - 1, 0), 1), :] * jnp.where(c > 0, 1.0, 0.0)
        next_row = src_ref[0, pl.ds(jnp.minimum(r0 + L, seq - 1), 1), :] * jnp.where(c < nc - 1, 1.0, 0.0)
        x_prev = jnp.where(row_id == 0, prev_row, pltpu.roll(x, 1, axis=0))
        x_next = jnp.where(row_id == L - 1, next_row, pltpu.roll(x, L - 1, axis=0))
        return _silu(x_prev * w[0:1, :] + x * w[1:2, :] + x_next * w[2:3, :])

    def prologue(c, carry):
        r0 = pl.multiple_of(c * L, L)
        qc_ref[pl.ds(r0, L), :] = (conv_silu(q_ref, conv_ref[0], r0, c) * (d ** -0.5)).astype(qc_ref.dtype)
        kc_ref[pl.ds(r0, L), :] = conv_silu(k_ref, conv_ref[1], r0, c)
        return carry

    lax.fori_loop(0, nc, prologue, 0)

    c_ref[...] = jnp.zeros_like(c_ref)
    m_ref[...] = jnp.zeros_like(m_ref)
    ones = jnp.ones((L, d), MXU_DTYPE)

    masks = [_scan_mask(L, reverse) for reverse in (False, True)]
    dst_refs = (hf_ref, hb_ref)
    cb = MLSTM_GROUP_CHUNKS
    sub_id = lax.broadcasted_iota(jnp.int32, (SUBLANES, L), 0)

    def row_cumsum(row, mask):
        hi = _mx(row).astype(F32)
        mid = _mx(row - hi).astype(F32)
        lo = row - hi - mid
        terms = jnp.where(sub_id == 0, hi, jnp.where(sub_id == 1, mid, jnp.where(sub_id == 2, lo, 0.0)))
        return jnp.sum(_dot_nt(terms, mask), axis=0, keepdims=True)

    def body(i, carry):
        chains = [(dr, j) for j in range(cb) for dr in range(2)]
        chunk_of = lambda dr, j: (i * cb + j) if dr == 0 else (nc - 1 - (i * cb + j))
        rows = [pl.ds(pl.multiple_of(chunk_of(dr, j) * L, L), L) for dr, j in chains]
        each = lambda f, *lists: [f(*args) for args in zip(*lists)]

        g = each(lambda r_: gate_ref[0, r_, :], rows)
        bcum = each(lambda dj, g_: _scan_cumsum(masks[dj[0]], g_, short_chunk=False), chains, g)
        ig_col = each(lambda dj, g_: _extract_col(g_, _GATE_IG + dj[0] * M_HEADS + head), chains, g)
        b_col = each(lambda dj, b_: _extract_col(b_, _GATE_FG + dj[0] * M_HEADS + head), chains, bcum)
        g_tot = each(lambda dj, b_: b_[0:1, :] if dj[0] == 1 else b_[L - 1:L, :], chains, b_col)
        ig_row = each(lambda dj, r_: gt_ref[pl.ds(_GATE_IG + dj[0] * M_HEADS + head, 1), r_], chains, rows)
        fg_row = each(lambda dj, r_: gt_ref[pl.ds(_GATE_FG + dj[0] * M_HEADS + head, 1), r_], chains, rows)
        b_row = each(lambda dj, f_: row_cumsum(f_, _mx(masks[dj[0]])), chains, fg_row)
        dmat = each(lambda dj, bc_, br_, ir_: jnp.where(masks[dj[0]], bc_ - br_ + ir_, NEG_BIG),
                    chains, b_col, b_row, ig_row)
        dmax = each(lambda d_: jnp.max(d_, axis=1, keepdims=True), dmat)
        q = each(lambda r_: qc_ref[r_, :], rows)
        k = each(lambda r_: kc_ref[r_, :], rows)
        v_aug = each(lambda r_: jnp.concatenate([_mx(v_ref[0, r_, :]), ones], axis=1), rows)
        sc = each(lambda q_, k_, d_, m_: _mx(_dot_nt(q_, k_) * jnp.exp(d_ - m_)), q, k, dmat, dmax)
        nd_intra = each(_dot, sc, v_aug)
        kw_log = each(lambda g_, b_, i_: g_ - b_ + i_, g_tot, b_col, ig_col)
        kw_max = each(lambda w_: jnp.max(w_, axis=0, keepdims=True), kw_log)
        d_c = each(lambda k_, w_, a_, v_: _dot_tn(_mx(k_ * jnp.exp(w_ - a_)), v_), k, kw_log, kw_max, v_aug)

        c_start = [None] * len(chains)
        m_start = [None] * len(chains)
        for dr in range(2):
            c_cur = c_ref[dr]
            m_cur = m_ref[dr][:, 0:1]
            for j in range(cb):
                idx = chains.index((dr, j))
                c_start[idx], m_start[idx] = _mx(c_cur), m_cur
                m_new = jnp.maximum(g_tot[idx] + m_cur, kw_max[idx])
                c_cur = jnp.exp(g_tot[idx] + m_cur - m_new) * c_cur + jnp.exp(kw_max[idx] - m_new) * d_c[idx]
                m_cur = m_new
            c_ref[dr] = c_cur
            m_ref[dr] = jnp.broadcast_to(m_cur, (1, LANES))

        nd_inter = each(_dot, q, c_start)
        for idx, (dr, j) in enumerate(chains):
            inter_log = b_col[idx] + m_start[idx]
            m_t = jnp.maximum(inter_log, dmax[idx])
            numden = jnp.exp(dmax[idx] - m_t) * nd_intra[idx] + jnp.exp(inter_log - m_t) * nd_inter[idx]
            dst_refs[dr][rows[idx], :] = numden[:, :d] / jnp.maximum(jnp.abs(numden[:, d:]), jnp.exp(-m_t))
        return carry

    lax.fori_loop(0, nc // cb, body, 0)

    def epilogue(c, carry):
        r0 = pl.multiple_of(c * L, L)
        h = hf_ref[pl.ds(r0, L), :] + hb_ref[pl.ds(r0, L), :]
        h = _rms(h, norm_ref[...])
        out_ref[0, pl.ds(r0, L), :] = h * jax.nn.sigmoid(o_ref[0, pl.ds(r0, L), :])
        return carry

    lax.fori_loop(0, nc, epilogue, 0)


def mlstm_mixer(p, gates_t, conv_qk, m_norm, *, chunk):
    b, t, _ = p.shape
    d = M_HEAD_DIM
    ngate = gates_t.shape[0]
    blk = lambda off: pl.BlockSpec((1, t, d), lambda i, h, off=off: (i, 0, off + h))
    conv = conv_qk.reshape(CONV_WIDTH, 2, M_HEADS, d).transpose(2, 1, 0, 3)
    return pl.pallas_call(
        functools.partial(_mlstm_kernel, seq=t, chunk=chunk),
        grid=(b, M_HEADS),
        in_specs=[blk(_EV_MQ), blk(_EV_MK), blk(_EV_MV), blk(_EV_MO),
                  pl.BlockSpec((1, t, LANES), lambda i, h: (i, 0, _EV_GATE)),
                  pl.BlockSpec((ngate, t), lambda i, h: (0, i)),
                  pl.BlockSpec((None, 2, CONV_WIDTH, d), lambda i, h: (h, 0, 0, 0)),
                  pl.BlockSpec((1, d), lambda i, h: (0, h))],
        out_specs=pl.BlockSpec((1, t, d), lambda i, h: (i, 0, h)),
        out_shape=jax.ShapeDtypeStruct((b, t, M_HEADS * d), F32),
        scratch_shapes=[pltpu.VMEM((t, d), MXU_DTYPE),
                        pltpu.VMEM((t, d), F32),
                        pltpu.VMEM((t, d), F32),
                        pltpu.VMEM((t, d), F32),
                        pltpu.VMEM((2, d, 2 * d), F32),
                        pltpu.VMEM((2, 1, LANES), F32)],
        compiler_params=_params("parallel", "parallel"),
        name="mlstm",
    )(p, p, p, p, p, gates_t, conv, m_norm.reshape(1, M_HEADS * d))


def _gla_kernel(q_ref, k_ref, v0_ref, v1_ref, g0_ref, g1_ref, gate_ref, w2_ref, b2_ref, norm_ref, out_ref,
                of_ref, ob_ref, s_ref, *, seq, chunk):
    L = chunk
    nc = seq // L
    dk2 = 2 * G_KEY_DIM
    dv = G_VAL_DIM
    lane_k = lax.broadcasted_iota(jnp.int32, (L, dk2), 1)
    head0 = (lane_k < G_KEY_DIM).astype(F32)
    head1 = 1.0 - head0
    s_row = lax.broadcasted_iota(jnp.int32, (2 * dv, dk2), 0)
    s_col = lax.broadcasted_iota(jnp.int32, (2 * dv, dk2), 1)
    s_mask = ((s_row < dv) == (s_col < G_KEY_DIM)).astype(F32)

    s_ref[...] = jnp.zeros_like(s_ref)
    masks = [_scan_mask(L, reverse) for reverse in (False, True)]
    masks2 = [jnp.concatenate([m, m], axis=0) for m in masks]
    dst_refs = (of_ref, ob_ref)
    cb = GLA_GROUP_CHUNKS

    def body(i, carry):
        chains = [(d, j) for j in range(cb) for d in range(2)]
        chunk_of = lambda d, j: (i * cb + j) if d == 0 else (nc - 1 - (i * cb + j))
        rows = [pl.ds(pl.multiple_of(chunk_of(d, j) * L, L), L) for d, j in chains]
        each = lambda f, *lists: [f(*args) for args in zip(*lists)]

        z = each(lambda dj, r_: _dot(gate_ref[0, r_, :], w2_ref[dj[0], 0]) + b2_ref[dj[0], 0], chains, rows)
        la = each(lambda z_: _log_sigmoid(z_) * (1.0 / G_DECAY_TAU), z)
        bcum = each(lambda dj, l_: _scan_cumsum(masks[dj[0]], l_), chains, la)
        g_tot = each(lambda dj, b_: b_[0:1, :] if dj[0] == 1 else b_[L - 1:L, :], chains, bcum)
        mid = each(lambda b_: b_[L // 2:L // 2 + 1, :], bcum)
        q = each(lambda r_: q_ref[0, r_, :] * (G_KEY_DIM ** -0.5), rows)
        k = each(lambda r_: k_ref[0, r_, :], rows)
        v = each(lambda r_: jnp.concatenate([_mx(v0_ref[0, r_, :]), _mx(v1_ref[0, r_, :])], axis=1), rows)
        q_in = each(lambda q_, b_, m_: q_ * jnp.exp(b_ - m_), q, bcum, mid)
        q_in2 = each(lambda q_: _mx(jnp.concatenate([q_ * head0, q_ * head1], axis=0)), q_in)
        k_in = each(lambda k_, b_, m_: _mx(k_ * jnp.exp(m_ - b_)), k, bcum, mid)
        scores = each(lambda dj, q_, k_: _mx(jnp.where(masks2[dj[0]], _dot_nt(q_, k_), 0.0)), chains, q_in2, k_in)
        o_intra = each(lambda a_, v_: jnp.concatenate([_dot(a_[:L], v_[:, :dv]), _dot(a_[L:], v_[:, dv:])], axis=1),
                       scores, v)
        q_hat = each(lambda q_, b_: _mx(q_ * jnp.exp(b_)), q, bcum)
        k_hat = each(lambda k_, b_, g_: _mx(k_ * jnp.exp(g_ - b_)), k, bcum, g_tot)
        d_s = each(lambda v_, k_: _dot_tn(v_, k_) * s_mask, v, k_hat)
        decay = each(jnp.exp, g_tot)

        s_at_start = [None] * len(chains)
        for d in range(2):
            s_cur = s_ref[d]
            for j in range(cb):
                idx = chains.index((d, j))
                s_at_start[idx] = _mx(s_cur)
                s_cur = s_cur * decay[idx] + d_s[idx]
            s_ref[d] = s_cur
        o_inter = each(_dot_nt, q_hat, s_at_start)
        for idx, (d, j) in enumerate(chains):
            dst_refs[d][rows[idx], :] = o_intra[idx] + o_inter[idx]
        return carry

    lax.fori_loop(0, nc // cb, body, 0)

    rows_out = L * cb

    def epilogue(c, carry):
        rows = pl.ds(pl.multiple_of(c * rows_out, rows_out), rows_out)
        o = of_ref[rows, :] + ob_ref[rows, :]
        g = jnp.concatenate([g0_ref[0, rows, :], g1_ref[0, rows, :]], axis=1)
        nrm = norm_ref[...]
        o = jnp.concatenate([_rms(o[:, :dv], nrm[:, :dv]), _rms(o[:, dv:], nrm[:, dv:])], axis=1)
        out_ref[0, rows, :] = o * _silu(g)
        return carry

    lax.fori_loop(0, nc // cb, epilogue, 0)


def gla_mixer(p, decay_w2, decay_b, g_norm, *, chunk):
    b, t, _ = p.shape
    dv = G_VAL_DIM
    pairs = G_HEADS // 2
    dk2 = 2 * G_KEY_DIM
    w2 = jnp.zeros((2, LANES, pairs * dk2), F32)
    for direction in range(2):
        lo = _GATE_LR + direction * G_DECAY_RANK
        w2 = w2.at[direction, lo:lo + G_DECAY_RANK].set(decay_w2[direction])
    w2 = _mx(w2.reshape(2, LANES, pairs, dk2).transpose(0, 2, 1, 3))
    b2 = decay_b.reshape(2, pairs, 1, dk2)
    blk = lambda off, mult: pl.BlockSpec((1, t, LANES), lambda i, h, off=off, mult=mult: (i, 0, off + mult * h))
    return pl.pallas_call(
        functools.partial(_gla_kernel, seq=t, chunk=chunk),
        grid=(b, pairs),
        in_specs=[blk(_EV_GQ, 1), blk(_EV_GK, 1), blk(_EV_GV, 2), blk(_EV_GV + 1, 2),
                  blk(_EV_GG, 2), blk(_EV_GG + 1, 2),
                  pl.BlockSpec((1, t, LANES), lambda i, h: (i, 0, _EV_GATE)),
                  pl.BlockSpec((2, 1, LANES, dk2), lambda i, h: (0, h, 0, 0)),
                  pl.BlockSpec((2, 1, 1, dk2), lambda i, h: (0, h, 0, 0)),
                  pl.BlockSpec((1, 2 * dv), lambda i, h: (0, h))],
        out_specs=pl.BlockSpec((1, t, 2 * dv), lambda i, h: (i, 0, h)),
        out_shape=jax.ShapeDtypeStruct((b, t, G_HEADS * dv), F32),
        scratch_shapes=[pltpu.VMEM((t, 2 * dv), F32),
                        pltpu.VMEM((t, 2 * dv), F32),
                        pltpu.VMEM((2, 2 * dv, dk2), F32)],
        compiler_params=_params("parallel", "parallel"),
        name="gla",
    )(p, p, p, p, p, p, p, w2, b2, g_norm.reshape(1, G_HEADS * dv))


def _rwkv_in_kernel(x_ref, xp_ref, xn_ref, nrm_ref, mu_ref, wrkv_ref, w1_ref, w2_ref, w0_ref,
                    a1_ref, a2_ref, a0_ref, g1_ref, g2_ref,
                    r_ref, k_ref, v_ref, a_ref, lwf_ref, lwb_ref, g_ref, *, tm):
    i = pl.program_id(1)
    nt = pl.num_programs(1)
    nrm = nrm_ref[...]
    h = _rms(x_ref[0], nrm)
    before = _rms(xp_ref[0, 0], nrm)[SUBLANES - 1:SUBLANES, :] * jnp.where(i > 0, 1.0, 0.0)
    after = _rms(xn_ref[0, 0], nrm)[0:1, :] * jnp.where(i < nt - 1, 1.0, 0.0)
    row = lax.broadcasted_iota(jnp.int32, h.shape, 0)
    h_prev = jnp.where(row == 0, before, pltpu.roll(h, 1, axis=0))
    h_next = jnp.where(row == tm - 1, after, pltpu.roll(h, tm - 1, axis=0))
    hh = 0.5 * (h_prev + h_next) - h
    mix = lambda j: h + hh * mu_ref[j:j + 1, :]
    xr, xw, xk, xv, xa, xg = (mix(j) for j in range(6))
    r_ref[0] = _dot(xr, wrkv_ref[0]).astype(r_ref.dtype)
    k_ref[0] = _dot(xk, wrkv_ref[1]).astype(k_ref.dtype)
    v_ref[0] = _dot(xv, wrkv_ref[2]).astype(v_ref.dtype)
    a_ref[0] = jax.nn.sigmoid(a0_ref[...] + _dot(_dot(xa, a1_ref[...]), a2_ref[...])).astype(a_ref.dtype)
    g_ref[0] = _dot(jax.nn.sigmoid(_dot(xg, g1_ref[...])), g2_ref[...]).astype(g_ref.dtype)
    for direction, dst in ((0, lwf_ref), (1, lwb_ref)):
        z = w0_ref[direction:direction + 1, :] + _dot(jnp.tanh(_dot(xw, w1_ref[direction])), w2_ref[direction])
        dst[0] = jax.nn.sigmoid(z) * -math.exp(-0.5)


def rwkv_in(x, norm, mu, w_rkv, w0, w1, w2, a0, a1, a2, g1, g2, *, tm):
    b, t, d = x.shape
    x8 = x.reshape(b, t // SUBLANES, SUBLANES, d)
    tb = tm // SUBLANES
    nb = t // SUBLANES
    rank_g = g1.shape[1]
    pad_g = -rank_g % LANES
    g1p = jnp.pad(g1, ((0, 0), (0, pad_g)))
    g2p = jnp.pad(g2, ((0, pad_g), (0, 0)))
    tile = pl.BlockSpec((1, tm, d), lambda i, j: (i, j, 0))
    out = [jax.ShapeDtypeStruct((b, t, d), dt) for dt in (MXU_DTYPE,) * 4 + (F32, F32, MXU_DTYPE)]
    weights = [norm.reshape(1, d), mu, _mx(w_rkv), _mx(w1), _mx(w2), w0,
               _mx(a1), _mx(a2), a0.reshape(1, d), _mx(g1p), _mx(g2p)]
    return pl.pallas_call(
        functools.partial(_rwkv_in_kernel, tm=tm),
        grid=(b, t // tm),
        in_specs=[tile,
                  pl.BlockSpec((1, 1, SUBLANES, d), lambda i, j: (i, jnp.maximum(j * tb - 1, 0), 0, 0)),
                  pl.BlockSpec((1, 1, SUBLANES, d), lambda i, j: (i, jnp.minimum((j + 1) * tb, nb - 1), 0, 0))]
                 + [_resident(w.shape) for w in weights],
        out_specs=[tile] * 7,
        out_shape=out,
        compiler_params=_params("parallel", "parallel"),
        name="rwkv_in",
    )(x, x8, x8, *weights)


def _rwkv_scan_kernel(r_ref, k_ref, v_ref, a_ref, lwf_ref, lwb_ref, kk_ref, ka_ref, rk_ref, lnw_ref, lnb_ref,
                      out_ref, y_ref, rq_ref, m_ref, sadd_ref, gam_ref, s_ref, *, seq, chunk):
    L = chunk
    nc = seq // L
    n = R_HEAD_DIM
    lane = lax.broadcasted_iota(jnp.int32, (L, 2 * n), 1)
    head0 = (lane < n).astype(F32)
    head1 = 1.0 - head0

    def head_sum(x):
        s0 = jnp.sum(x * head0, axis=1, keepdims=True)
        s1 = jnp.sum(x * head1, axis=1, keepdims=True)
        return s0 * head0 + s1 * head1

    head0_mx, head1_mx = _mx(head0), _mx(head1)

    def stack(x):
        x = _mx(x)
        return jnp.concatenate([x * head0_mx, x * head1_mx], axis=0)

    def load(c):
        r0 = pl.multiple_of(c * L, L)
        rows = pl.ds(r0, L)
        r, k, v, a = (ref[0, rows, :].astype(F32) for ref in (r_ref, k_ref, v_ref, a_ref))
        kk = k * kk_ref[...]
        kk = kk / jnp.maximum(jnp.sqrt(head_sum(kk * kk)), R_KK_NORM_FLOOR)
        kp = k * (1.0 + (a - 1.0) * ka_ref[...])
        return rows, r, kp, v, -kk, kk * a

    lane2 = lax.broadcasted_iota(jnp.int32, (L, 4 * n), 1)
    head0_2 = ((lane2 & (2 * n - 1)) < n).astype(F32)
    head0_2_mx, head1_2_mx = _mx(head0_2), _mx(1.0 - head0_2)

    def stack2(x):
        x = _mx(x)
        return jnp.concatenate([x * head0_2_mx, x * head1_2_mx], axis=0)

    masks_l = [_scan_mask(L, reverse) for reverse in (False, True)]
    row_cat = lax.broadcasted_iota(jnp.int32, (L, 2 * L), 0)
    col_cat = lax.broadcasted_iota(jnp.int32, (L, 2 * L), 1) & (L - 1)
    masks = [(col_cat <= row_cat, col_cat < row_cat), (col_cat >= row_cat, col_cat > row_cat)]
    eye_cat = (col_cat == row_cat).astype(F32)
    same_head = ((lax.broadcasted_iota(jnp.int32, (2 * n, 2 * n), 0) < n)
                 == (lax.broadcasted_iota(jnp.int32, (2 * n, 2 * n), 1) < n)).astype(F32)
    lw_refs = (lwf_ref, lwb_ref)
    cb = min(RWKV_BUILD_CHUNKS, nc // 2)
    groups = nc // cb
    chunk_of = lambda d, g, j: (g * cb + j) if d == 0 else (nc - 1 - (g * cb + j))
    each = lambda f, *lists: [f(*args) for args in zip(*lists)]

    def finish(c, y):
        rows = pl.ds(pl.multiple_of(c * L, L), L)
        r, k, v, a = (ref[0, rows, :].astype(F32) for ref in (r_ref, k_ref, v_ref, a_ref))
        kp = k * (1.0 + (a - 1.0) * ka_ref[...])
        mean = head_sum(y) * (1.0 / n)
        yc = y - mean
        var = head_sum(yc * yc) * (1.0 / n)
        yn = yc * lax.rsqrt(var + R_GN_EPS) * lnw_ref[...] + lnb_ref[...]
        out_ref[0, rows, :] = yn + head_sum(r * kp * rk_ref[...]) * v

    def sweep_steps(g, crossed):
        state = [s_ref[d] for d in range(2)]

        def step(j):
            cs = [chunk_of(d, g, j) for d in range(2)]
            rows = [pl.ds(pl.multiple_of(c * L, L), L) for c in cs]
            s_mx = each(_mx, state)
            y = [y_ref[d, rows[d], :] + _dot_nt(rq_ref[d, rows[d], :], s_mx[d]) for d in range(2)]
            for d in range(2):
                state[d] = (state[d] * gam_ref[d, pl.ds(cs[d], 1), :] + _dot(s_mx[d], m_ref[d, cs[d]])
                            + sadd_ref[d, cs[d]])
            if j == cb - 1:
                for d in range(2):
                    s_ref[d] = state[d]
            for d in range(2):
                if crossed:
                    finish(cs[d], y[d] + y_ref[1 - d, rows[d], :])
                else:
                    y_ref[d, rows[d], :] = y[d]

        return [functools.partial(step, j) for j in range(cb)]

    def build_group(g, interleaved=()):
        pending = list(interleaved)
        run_one = lambda: pending.pop(0)() if pending else None
        chains = [(d, j) for j in range(cb) for d in range(2)]
        chunk_ids = [chunk_of(d, g, j) for d, j in chains]
        loaded = [load(c) for c in chunk_ids]

        lw = each(lambda dj, ld: lw_refs[dj[0]][0, ld[0], :], chains, loaded)
        cin = each(lambda dj, x: _scan_cumsum(masks_l[dj[0]], x), chains, lw)
        ctot = each(lambda dj, c_: c_[0:1, :] if dj[0] == 1 else c_[L - 1:L, :], chains, cin)
        e_out = each(lambda c_: jnp.exp(-c_), cin)
        e_end = each(lambda c_, t_: jnp.exp(t_ - c_), cin, ctot)
        a_hat = each(lambda ld, c_, l_: ld[4] * jnp.exp(c_ - l_), loaded, cin, lw)
        r_hat = each(lambda ld, c_: ld[1] * jnp.exp(c_), loaded, cin)
        x = each(lambda a_, r_: _mx(jnp.concatenate([a_, r_], axis=0)), a_hat, r_hat)
        yk = each(lambda ld, e_: jnp.concatenate([stack(ld[5] * e_), stack(ld[2] * e_)], axis=0), loaded, e_out)
        run_one()
        gram = each(_dot_nt, x, yk)
        a_ab = each(lambda dj, g_: jnp.where(masks[dj[0]][1], g_[:L, :2 * L], 0.0), chains, gram)
        a_ak = each(lambda dj, g_: _mx(jnp.where(masks[dj[0]][1], g_[:L, 2 * L:], 0.0)), chains, gram)
        a_rb = each(lambda dj, g_: _mx(jnp.where(masks[dj[0]][0], g_[L:, :2 * L], 0.0)), chains, gram)
        a_rk = each(lambda dj, g_: _mx(jnp.where(masks[dj[0]][0], g_[L:, 2 * L:], 0.0)), chains, gram)

        t_inv = each(lambda n_: eye_cat + n_, a_ab)
        power = each(lambda n_: _mx(_dot(n_, stack(n_))), a_ab)
        for _ in range(L.bit_length() - 3):
            both = each(lambda p_, t_: _dot(p_, stack2(jnp.concatenate([p_, _mx(t_)], axis=1))), power, t_inv)
            power = each(lambda b_: _mx(b_[:, :2 * L]), both)
            t_inv = each(lambda t_, b_: t_ + b_[:, 2 * L:], t_inv, both)
            run_one()
        t_inv = each(lambda p_, t_: t_ + _dot(p_, stack(t_)), power, t_inv)

        v2 = each(lambda ld: stack(ld[3]), loaded)
        av = each(_dot, a_ak, v2)
        w = each(lambda t_, a_, av_: _dot(t_, stack2(jnp.concatenate([_mx(a_), _mx(av_)], axis=1))),
                 t_inv, a_hat, av)
        z = each(lambda b_, w_: _dot(b_, stack2(w_)), a_rb, w)
        y0 = each(lambda z_, k_, v_: z_[:, 2 * n:] + _dot(k_, v_), z, a_rk, v2)
        while pending:
            run_one()
        b_end = each(lambda ld, e_: _mx(ld[5] * e_), loaded, e_end)
        k_end = each(lambda ld, e_: _mx(ld[2] * e_), loaded, e_end)
        m_lr = each(lambda w_, b_: _dot_tn(w_[:, :2 * n], b_) * same_head, w, b_end)
        s_add = each(lambda w_, ld, b_, k_: _dot_tn(jnp.concatenate([w_[:, 2 * n:], ld[3]], axis=0),
                                                    jnp.concatenate([b_, k_], axis=0)) * same_head,
                     w, loaded, b_end, k_end)
        for idx, (d, j) in enumerate(chains):
            rows = loaded[idx][0]
            y_ref[d, rows, :] = y0[idx]
            rq_ref[d, rows, :] = (r_hat[idx] + z[idx][:, :2 * n]).astype(rq_ref.dtype)
            m_ref[d, chunk_ids[idx]] = m_lr[idx].astype(m_ref.dtype)
            sadd_ref[d, chunk_ids[idx]] = s_add[idx]
            gam_ref[d, pl.ds(chunk_ids[idx], 1), :] = jnp.exp(ctot[idx])

    s_ref[...] = jnp.zeros_like(s_ref)
    build_group(0)

    def before_crossing(g, carry):
        build_group(g, sweep_steps(g - 1, crossed=False))
        return carry

    def after_crossing(g, carry):
        build_group(g, sweep_steps(g - 1, crossed=True))
        return carry

    lax.fori_loop(1, groups // 2 + 1, before_crossing, 0)
    lax.fori_loop(groups // 2 + 1, groups, after_crossing, 0)
    for step in sweep_steps(groups - 1, crossed=groups > 1):
        step()


def rwkv_scan(r, k, v, a, lwf, lwb, k_k, k_a, r_k, ln_w, ln_b, *, chunk):
    b, t, d = r.shape
    w = 2 * R_HEAD_DIM
    pairs = d // w
    nc = t // chunk
    assert chunk == R_HEAD_DIM and nc % 2 == 0, (chunk, t)
    blk = pl.BlockSpec((1, t, w), lambda i, h: (i, 0, h))
    vec = pl.BlockSpec((1, w), lambda i, h: (0, h))
    row = lambda z: z.reshape(1, d)
    return pl.pallas_call(
        functools.partial(_rwkv_scan_kernel, seq=t, chunk=chunk),
        grid=(b, pairs),
        in_specs=[blk] * 6 + [vec] * 5,
        out_specs=blk,
        out_shape=jax.ShapeDtypeStruct((b, t, d), F32),
        scratch_shapes=[pltpu.VMEM((2, t, w), F32),
                        pltpu.VMEM((2, t, w), MXU_DTYPE),
                        pltpu.VMEM((2, nc, w, w), MXU_DTYPE),
                        pltpu.VMEM((2, nc, w, w), F32),
                        pltpu.VMEM((2, nc, w), F32),
                        pltpu.VMEM((2, w, w), F32)],
        compiler_params=_params("parallel", "parallel"),
        name="rwkv_scan",
    )(r, k, v, a, lwf, lwb, row(k_k), row(k_a), row(r_k), row(ln_w), row(ln_b))


def _xattn_kernel(x_ref, a_ref, b_ref, wm_ref, nrm_ref, wq_ref, k_ref, v_ref, wo_ref, o_ref, *, gated):
    if gated:
        merged = a_ref[0] * b_ref[0].astype(F32)
    else:
        merged = jnp.concatenate([_mx(a_ref[0]), _mx(b_ref[0])], axis=1)
    x = x_ref[0] + _dot(merged, wm_ref[...])
    d = x.shape[-1]
    dh = d // X_HEADS
    q = _mx(_dot(_rms(x, nrm_ref[...]), wq_ref[...]))
    cols = [slice(hd * dh, (hd + 1) * dh) for hd in range(X_HEADS)]
    s = [_dot_nt(q[:, c], k_ref[0, :, c]) * (dh ** -0.5) for c in cols]
    p = [jnp.exp(s_ - jnp.max(s_, axis=-1, keepdims=True)) for s_ in s]
    p = [_mx(p_ / jnp.sum(p_, axis=-1, keepdims=True)) for p_ in p]
    heads = [_mx(_dot(p_, v_ref[0, :, c])) for p_, c in zip(p, cols)]
    o_ref[0] = x + _dot(jnp.concatenate(heads, axis=1), wo_ref[...])


def mixer_out_cross_attention(x, a, b_in, w_mix, kv, norm, wq, wo, *, tm, gated):
    b, t, d = x.shape
    m = kv.shape[1]
    tile = lambda width: pl.BlockSpec((1, tm, width), lambda i, j: (i, j, 0))
    return pl.pallas_call(
        functools.partial(_xattn_kernel, gated=gated),
        grid=(b, t // tm),
        in_specs=[tile(d), tile(a.shape[-1]), tile(b_in.shape[-1]), _resident(w_mix.shape),
                  _resident((1, d)), _resident((d, d)),
                  pl.BlockSpec((1, m, d), lambda i, j: (i, 0, 0)),
                  pl.BlockSpec((1, m, d), lambda i, j: (i, 0, 1)),
                  _resident((d, d))],
        out_specs=tile(d),
        out_shape=jax.ShapeDtypeStruct((b, t, d), F32),
        compiler_params=_params("parallel", "parallel"),
        name="mixer_out_cross_attention",
    )(x, a, b_in, _mx(w_mix), norm.reshape(1, d), _mx(wq), kv, kv, _mx(wo))


def _ffn_kernel(x_ref, nrm_ref, wg_ref, wu_ref, wd_ref, fin_ref, o_ref, *, final_norm):
    x = x_ref[...]
    h = _mx(_rms(x, nrm_ref[...]))
    act = _silu(_dot(h, wg_ref[...])) * _dot(h, wu_ref[...])
    y = x + _dot(act, wd_ref[...])
    o_ref[...] = _rms(y, fin_ref[...]) if final_norm else y


def swiglu_ffn(x2d, norm, w_gate, w_up, w_down, final_gain, *, tm, final_norm):
    n, d = x2d.shape
    dff = w_gate.shape[1]
    row = pl.BlockSpec((tm, d), lambda i: (i, 0))
    return pl.pallas_call(
        functools.partial(_ffn_kernel, final_norm=final_norm),
        grid=(n // tm,),
        in_specs=[row, _resident((1, d)), _resident((d, dff)), _resident((d, dff)), _resident((dff, d)),
                  _resident((1, d))],
        out_specs=row,
        out_shape=jax.ShapeDtypeStruct((n, d), F32),
        compiler_params=_params("parallel"),
        name="swiglu_ffn",
    )(x2d, norm.reshape(1, d), _mx(w_gate), _mx(w_up), _mx(w_down), final_gain.reshape(1, d))


def _pack_even_in(w_in):
    d = w_in.shape[0]
    mw, gkw, gvw = M_HEADS * M_HEAD_DIM, G_HEADS * G_KEY_DIM, G_HEADS * G_VAL_DIM
    widths = (mw, mw, mw, mw, 2 * M_HEADS, 2 * M_HEADS, gkw, gkw, gvw, gvw, 2 * G_DECAY_RANK)
    starts = [0]
    for wdt in widths:
        starts.append(starts[-1] + wdt)
    mq, mk, mv, mo, mi, mf, gq, gk, gv, gg, glr = (w_in[:, starts[j]:starts[j + 1]] for j in range(len(widths)))
    gate = jnp.concatenate([mi, mf, glr], axis=1)
    gate = jnp.pad(gate, ((0, 0), (0, LANES - gate.shape[1])))
    packed = jnp.concatenate([mq, mk, mv, mo, gq, gk, gv, gg, gate], axis=1)
    assert packed.shape == (d, _EV_BLOCKS * LANES)
    gates_t = jnp.concatenate([mi, mf], axis=1).T
    return _mx(packed), _mx(gates_t)


def _even_mixer_layer(x, norm, w_in, conv_qk, ig_bias, fg_bias, m_norm, decay_w2, decay_b, g_norm, *, tm):
    b, t, d = x.shape
    x2d = x.reshape(b * t, d)
    w_packed, w_gates_t = _pack_even_in(w_in)
    gate_bias = jnp.concatenate([ig_bias.reshape(-1), fg_bias.reshape(-1)])
    p, gates_t = even_in_proj(x2d, norm, w_packed, w_gates_t, gate_bias, tm=tm)
    p = p.reshape(b, t, -1)
    hm = mlstm_mixer(p, gates_t, conv_qk, m_norm, chunk=min(MLSTM_CHUNK, t))
    og = gla_mixer(p, decay_w2, decay_b, g_norm, chunk=min(GLA_CHUNK, t))
    return hm, og


def _rwkv_mixer_layer(x, norm, mu, w_rkv, w0, w1, w2, a0, a1, a2, g1, g2, k_k, k_a, r_k, ln_w, ln_b, *, tm):
    b, t, d = x.shape
    r, k, v, a, lwf, lwb, g = rwkv_in(x, norm, mu, w_rkv, w0, w1, w2, a0, a1, a2, g1, g2, tm=tm)
    y = rwkv_scan(r, k, v, a, lwf, lwb, k_k, k_a, r_k.reshape(-1), ln_w, ln_b, chunk=min(RWKV_CHUNK, t))
    return y, g


def kernel(x, mem, norm_mix, norm_xattn, norm_mem, norm_ffn, norm_final, xa_wq, xa_wkv, xa_wo, ffn_w_gate, ffn_w_up, ffn_w_down, ev_w_in, ev_conv_qk, ev_m_ig_bias, ev_m_fg_bias, ev_m_norm, ev_g_decay_w2, ev_g_decay_b, ev_g_norm, ev_w_out, od_mu, od_w_rkv, od_w0, od_w1, od_w2, od_a0, od_a1, od_a2, od_g1, od_g2, od_k_k, od_k_a, od_r_k, od_ln_w, od_ln_b, od_w_o):
    b, t, d = x.shape
    m = mem.shape[1]
    depth = norm_mix.shape[0]
    tm = min(TOKEN_TILE, t)
    for layer in range(depth):
        if layer % 2 == 0:
            e = layer // 2
            mix_a, mix_b = _even_mixer_layer(x, norm_mix[layer], ev_w_in[e], ev_conv_qk[e], ev_m_ig_bias[e],
                                             ev_m_fg_bias[e], ev_m_norm[e], ev_g_decay_w2[e], ev_g_decay_b[e],
                                             ev_g_norm[e], tm=tm)
            w_mix = ev_w_out[e]
        else:
            o = layer // 2
            mix_a, mix_b = _rwkv_mixer_layer(x, norm_mix[layer], od_mu[o], od_w_rkv[o], od_w0[o], od_w1[o], od_w2[o],
                                             od_a0[o], od_a1[o], od_a2[o], od_g1[o], od_g2[o], od_k_k[o], od_k_a[o],
                                             od_r_k[o], od_ln_w[o], od_ln_b[o], tm=tm)
            w_mix = od_w_o[o]
        kv = rms_matmul(mem.reshape(b * m, d), norm_mem[layer], _mx(xa_wkv[layer]), tm=min(TOKEN_TILE, b * m),
                        out_dtype=MXU_DTYPE).reshape(b, m, 2 * d)
        x = mixer_out_cross_attention(x, mix_a, mix_b, w_mix, kv, norm_xattn[layer], xa_wq[layer], xa_wo[layer],
                                      tm=min(XATTN_TOKEN_TILE, t), gated=layer % 2 == 1)
        last = layer == depth - 1
        x = swiglu_ffn(x.reshape(b * t, d), norm_ffn[layer], ffn_w_gate[layer], ffn_w_up[layer], ffn_w_down[layer],
                       norm_final, tm=tm, final_norm=last).reshape(b, t, d)
    return x
```
